```python
import functools
import jax
import jax.numpy as jnp
from jax import lax
import numpy as np

D_MODEL = 1024
BATCH = 1
SEQ = 16384
DEPTH = 2
DEC_BATCH = 32
DEC_SEQ = 8
PAST_LEN = 16384
PAGE_SIZE = 128

N_META = 16
HEAD_DIM = 64
D_A = D_MODEL // 2
D_B = D_MODEL - D_A
H_A = D_A // HEAD_DIM
H_B = D_B // HEAD_DIM
LORA_W = 64
LORA_A = 64
LORA_G = 128
A_K = D_A
A_V = 2 * D_A
A_W = 3 * D_A
A_A = A_W + LORA_W
A_G = A_A + LORA_A
A_IN = A_G + LORA_G
B_IN = 3 * D_B + H_B
IN_WIDTH = A_IN + B_IN
GN_EPS = HEAD_DIM * 1e-5
ATTN_BLOCK = 128
POOL_WINDOWS = (2, 4, 8, 16)
POOL_NG = len(POOL_WINDOWS)
POOL_G = D_MODEL // POOL_NG
POOL_BUF = max(POOL_WINDOWS) - 1
PEER_HEADS = 8
PEER_NKEYS = 128
PEER_EXPERTS = PEER_NKEYS * PEER_NKEYS
PEER_TOPK = 16
PEER_DQ = 256
PEER_HALF = PEER_DQ // 2
PEER_BLOCK = 128
LN_EPS = 1e-5
N_EVEN = (DEPTH + 1) // 2
N_ODD = DEPTH // 2
DN_ALPHA = (2 * DEPTH) ** 0.25
DN_BETA = (8 * DEPTH) ** -0.25
NEG = -1e30

kernel_name = 'hybrid_rwkv7_fox_pool_peer_step'


def _heads(t):
    return t.reshape(t.shape[:-1] + (t.shape[-1] // HEAD_DIM, HEAD_DIM))


def layer_norm(x, g, b):
    xf = x.astype(jnp.float32)
    mu = jnp.mean(xf, axis=-1, keepdims=True)
    var = jnp.mean(jnp.square(xf - mu), axis=-1, keepdims=True)
    return ((xf - mu) * lax.rsqrt(var + LN_EPS) * g + b).astype(x.dtype)


def deepnorm(x, sub, g, b):
    return layer_norm(DN_ALPHA * x + sub.astype(x.dtype), g, b)


def rwkv7_mix(pa, shift_prev, wkv0, mu_shift, w0, w2, a0, a2, g2, k_k, k_a, r_k, gn_g, gn_b):
    f32 = jnp.float32
    B, T, _ = pa.shape
    prev = jnp.concatenate([shift_prev[:, None, :].astype(pa.dtype), pa[:, :-1]], axis=1)
    xs = (pa + (prev - pa) * mu_shift).astype(f32)
    r, k, v, xw, xa, xg = jnp.split(xs, [A_K, A_V, A_W, A_A, A_G], axis=-1)
    w = -jax.nn.softplus(-(w0 + jnp.tanh(xw) @ w2)) - 0.5
    decay = jnp.exp(-jnp.exp(w))
    a = jax.nn.sigmoid(a0 + xa @ a2)
    g = jax.nn.sigmoid(xg) @ g2
    kk = _heads(k * k_k)
    kk = kk / jnp.maximum(jnp.sqrt(jnp.sum(kk * kk, axis=-1, keepdims=True)), 1e-12)
    k = k * (1.0 + (a - 1.0) * k_a)
    r_h, k_h, v_h, a_h, d_h = (_heads(t) for t in (r, k, v, a, decay))

    def step(S, inp):
        r_t, k_t, v_t, kk_t, a_t, d_t = inp
        sk = jnp.einsum('bhvk,bhk->bhv', S, kk_t)
        S = (S * d_t[:, :, None, :]
             - sk[..., None] * (kk_t * a_t)[:, :, None, :]
             + v_t[..., None] * k_t[:, :, None, :])
        return S, jnp.einsum('bhvk,bhk->bhv', S, r_t)

    seq = tuple(jnp.swapaxes(t, 0, 1) for t in (r_h, k_h, v_h, kk, a_h, d_h))
    S, y = lax.scan(step, wkv0.astype(f32), seq)
    y = jnp.swapaxes(y, 0, 1)
    m = jnp.mean(y, axis=-1, keepdims=True)
    var = jnp.mean(jnp.square(y - m), axis=-1, keepdims=True)
    yn = ((y - m) * lax.rsqrt(var + GN_EPS)).reshape(B, T, D_A) * gn_g + gn_b
    bonus = (jnp.sum(r_h * k_h * r_k, axis=-1, keepdims=True) * v_h).reshape(B, T, D_A)
    out = (yn + bonus) * g
    return out.astype(pa.dtype), S.astype(wkv0.dtype), pa[:, -1]


def fox_prompt(q, k, v, logf):
    B, T = q.shape[:2]
    nb = -(-T // ATTN_BLOCK)
    tp = nb * ATTN_BLOCK
    pad = tp - T
    q, k, v = (jnp.pad(t, ((0, 0), (0, pad), (0, 0), (0, 0))) for t in (q, k, v))
    c = jnp.cumsum(jnp.pad(logf, ((0, 0), (0, pad), (0, 0))), axis=1)
    c_t = jnp.swapaxes(c, 1, 2)
    kpos = jnp.arange(tp)
    scale = HEAD_DIM ** -0.5
    qb = jnp.moveaxis(q.reshape(B, nb, ATTN_BLOCK, H_B, HEAD_DIM), 1, 0)
    cb = jnp.moveaxis(c_t.reshape(B, H_B, nb, ATTN_BLOCK), 2, 0)
    qpos = kpos.reshape(nb, ATTN_BLOCK)

    def block(args):
        qi, ci, pi = args
        s = jnp.einsum('bqhd,bkhd->bhqk', qi, k, preferred_element_type=jnp.float32) * scale
        s = s + ci[..., :, None] - c_t[:, :, None, :]
        s = jnp.where(kpos[None, None, None, :] <= pi[None, None, :, None], s, NEG)
        p = jax.nn.softmax(s, axis=-1)
        return jnp.einsum('bhqk,bkhd->bqhd', p.astype(v.dtype), v)

    o = lax.map(block, (qb, cb, qpos))
    return jnp.moveaxis(o, 0, 1).reshape(B, tp, D_B)[:, :T]


def fox_sample(q, k, v, logf, cache_k, cache_v, cache_logf, page_table):
    S = q.shape[1]
    scale = HEAD_DIM ** -0.5
    causal = jnp.tril(jnp.ones((S, S), dtype=bool))

    def one(args):
        qi, ki, vi, lfi, pages = args
        kp = cache_k[pages].reshape(-1, H_B, HEAD_DIM)
        vp = cache_v[pages].reshape(-1, H_B, HEAD_DIM)
        lp = cache_logf[pages].reshape(-1, H_B).astype(jnp.float32)
        cp = jnp.cumsum(lp, axis=0)
        d_past = (cp[-1] - cp).T
        cn = jnp.cumsum(lfi, axis=0).T
        s_past = (jnp.einsum('qhd,khd->hqk', qi, kp, preferred_element_type=jnp.float32) * scale
                  + cn[:, :, None] + d_past[:, None, :])
        s_new = (jnp.einsum('qhd,khd->hqk', qi, ki, preferred_element_type=jnp.float32) * scale
                 + cn[:, :, None] - cn[:, None, :])
        s_new = jnp.where(causal[None], s_new, NEG)
        p = jax.nn.softmax(jnp.concatenate([s_past, s_new], axis=-1), axis=-1)
        n_past = kp.shape[0]
        o = (jnp.einsum('hqk,khd->qhd', p[..., :n_past].astype(vp.dtype), vp)
             + jnp.einsum('hqk,khd->qhd', p[..., n_past:].astype(vi.dtype), vi))
        return o.reshape(S, D_B)

    return lax.map(one, (q, k, v, logf, page_table))


def even_mixer(x, shift_prev, wkv0, attend, w_in, b_f, w_o, mu_shift, w0, w2, a0, a2, g2,
               k_k, k_a, r_k, gn_g, gn_b):
    proj = jnp.einsum('btd,de->bte', x, w_in)
    pa, pb = proj[..., :A_IN], proj[..., A_IN:]
    ya, wkv_new, shift_new = rwkv7_mix(pa, shift_prev, wkv0, mu_shift, w0, w2, a0, a2, g2,
                                       k_k, k_a, r_k, gn_g, gn_b)
    q, k, v, fl = jnp.split(pb, [D_B, 2 * D_B, 3 * D_B], axis=-1)
    q, k, v = _heads(q), _heads(k), _heads(v)
    logf = jax.nn.log_sigmoid((fl + b_f).astype(jnp.float32))
    yb = attend(q, k, v, logf)
    y = jnp.einsum('bte,ed->btd', jnp.concatenate([ya, yb.astype(ya.dtype)], axis=-1), w_o)
    return y, (k, v, logf, wkv_new, shift_new)


def pool_mixer(x, buf, n_prev, w_pool, pool_scale):
    B, T, D = x.shape
    ext = jnp.concatenate([buf.astype(x.dtype), x], axis=1)
    cs = jnp.cumsum(jnp.pad(ext.astype(jnp.float32), ((0, 0), (1, 0), (0, 0))), axis=1)
    end = POOL_BUF + 1
    t_idx = jnp.arange(T)
    means = []
    for gi, win in enumerate(POOL_WINDOWS):
        c = cs[..., gi * POOL_G:(gi + 1) * POOL_G]
        tot = c[:, end:end + T] - c[:, end - win:end - win + T]
        cnt = jnp.minimum(win, n_prev + t_idx + 1).astype(jnp.float32)
        means.append(tot / cnt[None, :, None])
    diff = (jnp.concatenate(means, axis=-1) - x.astype(jnp.float32)).reshape(B, T, POOL_NG, POOL_G)
    y = jnp.einsum('btgc,gcd->btgd', diff, w_pool).reshape(B, T, D) * pool_scale
    return y.astype(x.dtype), ext[:, -POOL_BUF:]


def peer_ffn(x, wq, keys, u, v):
    B, T, D = x.shape
    n = B * T
    nb = -(-n // PEER_BLOCK)
    xb = jnp.pad(x.reshape(n, D), ((0, nb * PEER_BLOCK - n), (0, 0))).reshape(nb, PEER_BLOCK, D)

    def block(xi):
        q = (xi @ wq).reshape(-1, PEER_HEADS, 2, PEER_HALF)
        s = jnp.einsum('nhpc,hpkc->nhpk', q, keys, preferred_element_type=jnp.float32)
        sv, si = lax.top_k(s, PEER_TOPK)
        cand = sv[:, :, 0, :, None] + sv[:, :, 1, None, :]
        cv, ci = lax.top_k(cand.reshape(cand.shape[0], PEER_HEADS, PEER_TOPK * PEER_TOPK), PEER_TOPK)
        i1 = jnp.take_along_axis(si[:, :, 0], ci // PEER_TOPK, axis=-1)
        i2 = jnp.take_along_axis(si[:, :, 1], ci % PEER_TOPK, axis=-1)
        eidx = i1 * PEER_NKEYS + i2
        gate = jax.nn.softmax(cv, axis=-1)
        h = jnp.einsum('nhkd,nd->nhk', u[eidx], xi, preferred_element_type=jnp.float32)
        act = gate * jax.nn.gelu(h, approximate=False)
        return jnp.einsum('nhk,nhkd->nd', act.astype(v.dtype), v[eidx])

    return lax.map(block, xb).reshape(-1, D)[:n].reshape(B, T, D).astype(x.dtype)


def setup_inputs(seed: int = 0) -> dict:
    key = jax.random.key(seed)
    keys = jax.random.split(key, 48)
    counter = [0]

    def nk():
        counter[0] += 1
        return keys[counter[0] - 1]

    def nrm(shape, scale=1.0):
        return scale * jax.random.normal(nk(), shape, jnp.float32)

    def unif(shape, lo, hi):
        return jax.random.uniform(nk(), shape, jnp.float32, lo, hi)

    n_pages = PAST_LEN // PAGE_SIZE
    n_used = DEC_BATCH * n_pages
    n_pool = n_used + max(1, n_used // 4)
    page_table = jax.random.permutation(nk(), n_pool)[:n_used].reshape(DEC_BATCH, n_pages).astype(jnp.int32)
    return {
        'x_prompt': nrm((BATCH, SEQ, D_MODEL)),
        'x_sample': nrm((DEC_BATCH, DEC_SEQ, D_MODEL)),
        'cache_k': nrm((N_EVEN, n_pool, PAGE_SIZE, H_B, HEAD_DIM)),
        'cache_v': nrm((N_EVEN, n_pool, PAGE_SIZE, H_B, HEAD_DIM)),
        'cache_logf': jax.nn.log_sigmoid(3.0 + nrm((N_EVEN, n_pool, PAGE_SIZE, H_B))),
        'page_table': page_table,
        'state_wkv': nrm((N_EVEN, DEC_BATCH, H_A, HEAD_DIM, HEAD_DIM), 0.5),
        'state_shift': nrm((N_EVEN, DEC_BATCH, A_IN)),
        'state_pool': nrm((N_ODD, DEC_BATCH, POOL_BUF, D_MODEL)),
        'meta': nrm((N_META, D_MODEL)),
        'w_in': nrm((N_EVEN, D_MODEL, IN_WIDTH), D_MODEL ** -0.5),
        'b_f': 3.0 + nrm((N_EVEN, H_B), 0.5),
        'w_o': nrm((N_EVEN, D_MODEL, D_MODEL), D_MODEL ** -0.5 * DN_BETA),
        'mu_shift': unif((N_EVEN, A_IN), 0.0, 1.0),
        'w0': unif((N_EVEN, D_A), -6.0, -1.0),
        'w2': nrm((N_EVEN, LORA_W, D_A), LORA_W ** -0.5),
        'a0': nrm((N_EVEN, D_A), 0.1),
        'a2': nrm((N_EVEN, LORA_A, D_A), LORA_A ** -0.5),
        'g2': nrm((N_EVEN, LORA_G, D_A), LORA_G ** -0.5),
        'k_k': 0.85 + nrm((N_EVEN, D_A), 0.05),
        'k_a': 1.0 + nrm((N_EVEN, D_A), 0.05),
        'r_k': nrm((N_EVEN, H_A, HEAD_DIM), 0.1),
        'gn_g': 1.0 + nrm((N_EVEN, D_A), 0.05),
        'gn_b': nrm((N_EVEN, D_A), 0.05),
        'w_pool': nrm((N_ODD, POOL_NG, POOL_G, POOL_G), POOL_G ** -0.5 * DN_BETA),
        'pool_scale': 1.0 + nrm((N_ODD, D_MODEL), 0.1),
        'ln_g': 1.0 + nrm((DEPTH, 2, D_MODEL), 0.05),
        'ln_b': nrm((DEPTH, 2, D_MODEL), 0.05),
        'peer_wq': nrm((DEPTH, D_MODEL, PEER_HEADS * PEER_DQ), D_MODEL ** -0.5),
        'peer_keys': nrm((DEPTH, PEER_HEADS, 2, PEER_NKEYS, PEER_HALF), PEER_HALF ** -0.5),
        'peer_u': nrm((DEPTH, PEER_EXPERTS, D_MODEL), D_MODEL ** -0.5),
        'peer_v': nrm((DEPTH, PEER_EXPERTS, D_MODEL), DN_BETA),
    }


def reference(x_prompt, x_sample, cache_k, cache_v, cache_logf, page_table, state_wkv, state_shift,
              state_pool, meta, w_in, b_f, w_o, mu_shift, w0, w2, a0, a2, g2, k_k, k_a, r_k, gn_g, gn_b,
              w_pool, pool_scale, ln_g, ln_b, peer_wq, peer_keys, peer_u, peer_v):

    def run(h, attends, shift0, wkv0, pool0, n_prev):
        ks, vs, lfs, wkvs, shifts, pools = [], [], [], [], [], []
        for i in range(DEPTH):
            j = i // 2
            if i % 2 == 0:
                m, (k, v, lf, wkv, sh) = even_mixer(
                    h, shift0[j], wkv0[j], attends[j], w_in[j], b_f[j], w_o[j], mu_shift[j], w0[j],
                    w2[j], a0[j], a2[j], g2[j], k_k[j], k_a[j], r_k[j], gn_g[j], gn_b[j])
                ks.append(k)
                vs.append(v)
                lfs.append(lf)
                wkvs.append(wkv)
                shifts.append(sh)
            else:
                m, buf = pool_mixer(h, pool0[j], n_prev, w_pool[j], pool_scale[j])
                pools.append(buf)
            h = deepnorm(h, m, ln_g[i, 0], ln_b[i, 0])
            h = deepnorm(h, peer_ffn(h, peer_wq[i], peer_keys[i], peer_u[i], peer_v[i]), ln_g[i, 1], ln_b[i, 1])
        return (h, jnp.stack(ks), jnp.stack(vs), jnp.stack(lfs), jnp.stack(wkvs),
                jnp.stack(shifts), jnp.stack(pools))

    bp = x_prompt.shape[0]
    h_p = jnp.concatenate(
        [jnp.broadcast_to(meta.astype(x_prompt.dtype)[None], (bp, N_META, D_MODEL)), x_prompt], axis=1)
    h_p, k_p, v_p, logf_p, wkv_p, shift_p, pool_p = run(
        h_p, [fox_prompt] * N_EVEN,
        jnp.zeros((N_EVEN, bp, A_IN), x_prompt.dtype),
        jnp.zeros((N_EVEN, bp, H_A, HEAD_DIM, HEAD_DIM), x_prompt.dtype),
        jnp.zeros((N_ODD, bp, POOL_BUF, D_MODEL), x_prompt.dtype), 0)
    y_prompt = h_p[:, N_META:]

    attends_s = [functools.partial(fox_sample, cache_k=cache_k[j], cache_v=cache_v[j],
                                   cache_logf=cache_logf[j], page_table=page_table)
                 for j in range(N_EVEN)]
    y_sample, k_s, v_s, logf_s, wkv_s, shift_s, pool_s = run(
        x_sample, attends_s, state_shift, state_wkv, state_pool, POOL_BUF)

    return (y_prompt, y_sample, k_p, v_p, logf_p, wkv_p, shift_p, pool_p,
            k_s, v_s, logf_s, wkv_s, shift_s, pool_s)
```

```python
import functools
import math

import jax
import jax.numpy as jnp
from jax import lax
from jax.experimental import pallas as pl
from jax.experimental.pallas import tpu as pltpu

f32 = jnp.float32
bf16 = jnp.bfloat16

D_MODEL = 1024
N_META = 16
HEAD_DIM = 64
D_A = D_MODEL // 2
D_B = D_MODEL - D_A
H_A = D_A // HEAD_DIM
H_B = D_B // HEAD_DIM
LORA_W = 64
LORA_A = 64
LORA_G = 128
A_K = D_A
A_V = 2 * D_A
A_W = 3 * D_A
A_IN = A_W + LORA_W + LORA_A + LORA_G
GN_EPS = HEAD_DIM * 1e-5
POOL_WINDOWS = (2, 4, 8, 16)
POOL_G = D_MODEL // len(POOL_WINDOWS)
POOL_BUF = max(POOL_WINDOWS) - 1
PEER_HEADS = 8
PEER_NKEYS = 128
PEER_TOPK = 16
PEER_HALF = 128
LN_EPS = 1e-5
NEG = -1e30

LANES = 128
PAIR = 2 * HEAD_DIM
N_PAIR = D_A // PAIR
ROW_TILE = 512
SCAN_C = 64
INV_BLOCK = 16
FOX_TQ = 256
FOX_TK = 256
PAGES_PER_STEP = 8
PEER_TN = 256
PEER_TE = 512
VMEM_LIMIT = 56 * 1024 * 1024


def _cparams(*sem):
    return pltpu.CompilerParams(dimension_semantics=sem, vmem_limit_bytes=VMEM_LIMIT)


def _dot(a, b, dims=(((1,), (0,)), ((), ()))):
    return lax.dot_general(a, b, dims, preferred_element_type=f32)


NT = (((1,), (1,)), ((), ()))
TN = (((0,), (0,)), ((), ()))
NN = (((1,), (0,)), ((), ()))


def _split(x):
    hi = x.astype(bf16)
    lo = (x - hi.astype(f32)).astype(bf16)
    return hi, lo


def _dot3(a, b, dims=NN):
    ah, al = _split(a)
    bh, bl = _split(b)
    return _dot(ah, bh, dims) + (_dot(al, bh, dims) + _dot(ah, bl, dims))


def _dot_exact_rhs(a, b_bf, dims=NN):
    a0 = a.astype(bf16)
    r1 = a - a0.astype(f32)
    a1 = r1.astype(bf16)
    a2 = (r1 - a1.astype(f32)).astype(bf16)
    return _dot(a0, b_bf, dims) + (_dot(a1, b_bf, dims) + _dot(a2, b_bf, dims))


def _dot_exact_lhs(a_bf, b, dims=NN):
    b0 = b.astype(bf16)
    r1 = b - b0.astype(f32)
    b1 = r1.astype(bf16)
    b2 = (r1 - b1.astype(f32)).astype(bf16)
    return _dot(a_bf, b0, dims) + (_dot(a_bf, b1, dims) + _dot(a_bf, b2, dims))


def _layer_norm(x, g, b):
    mu = jnp.mean(x, axis=-1, keepdims=True)
    xc = x - mu
    var = jnp.mean(xc * xc, axis=-1, keepdims=True)
    return xc * lax.rsqrt(var + LN_EPS) * g + b


def _head_ones():
    r = lax.broadcasted_iota(jnp.int32, (D_A, D_A), 0) // HEAD_DIM
    c = lax.broadcasted_iota(jnp.int32, (D_A, D_A), 1) // HEAD_DIM
    return jnp.where(r == c, 1.0, 0.0).astype(bf16)


def _proj_kernel(x_ref, w_ref, bf_ref, pa_ref, kv_ref, qkv16_ref, lf_ref):
    x = x_ref[...].astype(bf16)
    y = _dot(x, w_ref[...])
    pa_ref[...] = y[:, :A_IN]
    q = y[:, A_IN:A_IN + D_B]
    kv = y[:, A_IN + D_B:A_IN + 3 * D_B]
    kv_ref[...] = kv
    qkv16_ref[:, :D_B] = (q * (HEAD_DIM ** -0.5)).astype(bf16)
    qkv16_ref[:, D_B:] = kv.astype(bf16)
    lf_ref[...] = jax.nn.log_sigmoid(y[:, A_IN + 3 * D_B:] + bf_ref[...])


def _in_projection(x, w_bf, b_f_pad):
    n = x.shape[0]
    nw = w_bf.shape[1]
    return pl.pallas_call(
        _proj_kernel,
        grid=(n // ROW_TILE,),
        in_specs=[
            pl.BlockSpec((ROW_TILE, D_MODEL), lambda i: (i, 0)),
            pl.BlockSpec((D_MODEL, nw), lambda i: (0, 0)),
            pl.BlockSpec((1, LANES), lambda i: (0, 0)),
        ],
        out_specs=[
            pl.BlockSpec((ROW_TILE, A_IN), lambda i: (i, 0)),
            pl.BlockSpec((ROW_TILE, 2 * D_B), lambda i: (i, 0)),
            pl.BlockSpec((ROW_TILE, 3 * D_B), lambda i: (i, 0)),
            pl.BlockSpec((ROW_TILE, LANES), lambda i: (i, 0)),
        ],
        out_shape=[
            jax.ShapeDtypeStruct((n, A_IN), f32),
            jax.ShapeDtypeStruct((n, 2 * D_B), f32),
            jax.ShapeDtypeStruct((n, 3 * D_B), bf16),
            jax.ShapeDtypeStruct((n, LANES), f32),
        ],
        compiler_params=_cparams("parallel"),
        name="in_projection",
    )(x, w_bf, b_f_pad)


PREP_TM = 256


def _prep_kernel(n_sample, n_valid, dec_seq,
                 pa_ref, prev_ref, ss_ref, mu_ref, vec_ref, w2_ref, a2_ref, g2_ref,
                 r_o, k_o, v_o, kk_o, b_o, lw_o, g_o, bonus_o):
    i = pl.program_id(0)
    tm = pa_ref.shape[0]
    pa = pa_ref[...]
    local = lax.broadcasted_iota(jnp.int32, (tm, 1), 0)
    row = i * tm + local
    prev = pltpu.roll(pa, 1, 0)
    prev = jnp.where(local == 0, prev_ref[7:8, :], prev)
    prev = jnp.where(row == n_sample, 0.0, prev)
    seq_start = jnp.logical_and(row < n_sample, row % dec_seq == 0)
    prev = jnp.where(seq_start, ss_ref[...], prev)
    xs = pa + (prev - pa) * mu_ref[...]

    w0 = vec_ref[0:1, :]
    a0 = vec_ref[1:2, :]
    k_k = vec_ref[2:3, :]
    k_a = vec_ref[3:4, :]
    r_k = vec_ref[4:5, :]
    r = xs[:, :A_K]
    k = xs[:, A_K:A_V]
    v = xs[:, A_V:A_W]
    wa = xs[:, A_W:A_W + LORA_W + LORA_A]
    xg = xs[:, A_W + LORA_W + LORA_A:]
    w = -jax.nn.softplus(-(w0 + _dot3(jnp.tanh(wa), w2_ref[...]))) - 0.5
    lw = -jnp.exp(w)
    a = jax.nn.sigmoid(a0 + _dot3(wa, a2_ref[...]))
    g = _dot(jax.nn.sigmoid(xg).astype(bf16), g2_ref[...].astype(bf16))
    ones = _head_ones()
    kk = k * k_k
    kk = kk / jnp.maximum(jnp.sqrt(_dot_exact_rhs(kk * kk, ones)), 1e-12)
    k = k * (1.0 + (a - 1.0) * k_a)
    bonus = _dot_exact_rhs(r * k * r_k, ones) * v
    valid = row < n_valid
    r_o[...] = r
    k_o[...] = jnp.where(valid, k, 0.0)
    v_o[...] = jnp.where(valid, v, 0.0)
    kk_o[...] = kk
    b_o[...] = jnp.where(valid, kk * a, 0.0)
    lw_o[...] = jnp.where(valid, lw, 0.0)
    g_o[...] = g
    bonus_o[...] = bonus


def _rwkv_prep(pa, ss_rows, mu, vecs, w2p, a2p, g2, n_sample, n_valid, dec_seq):
    n = pa.shape[0]
    assert n_sample == PREP_TM and n % PREP_TM == 0
    row_spec = pl.BlockSpec((PREP_TM, D_A), lambda i: (i, 0))
    const = lambda shape: pl.BlockSpec(shape, lambda i: (0, 0))
    return pl.pallas_call(
        functools.partial(_prep_kernel, n_sample, n_valid, dec_seq),
        grid=(n // PREP_TM,),
        in_specs=[
            pl.BlockSpec((PREP_TM, A_IN), lambda i: (i, 0)),
            pl.BlockSpec((8, A_IN), lambda i: (jnp.maximum(i * (PREP_TM // 8) - 1, 0), 0)),
            const((PREP_TM, A_IN)),
            const((1, A_IN)),
            const((8, D_A)),
            const((LORA_W + LORA_A, D_A)),
            const((LORA_W + LORA_A, D_A)),
            const((LORA_G, D_A)),
        ],
        out_specs=[row_spec] * 8,
        out_shape=[jax.ShapeDtypeStruct((n, D_A), f32)] * 8,
        compiler_params=_cparams("parallel"),
        name="rwkv_prep",
    )(pa, pa, ss_rows, mu, vecs, w2p, a2p, g2)


def _stack_heads(x, lane_a):
    return jnp.concatenate([jnp.where(lane_a, x, 0.0), jnp.where(lane_a, 0.0, x)], axis=0)


def _unit_lower_inverse(x):
    n = x.shape[0]
    ri = lax.broadcasted_iota(jnp.int32, (n, n), 0)
    ci = lax.broadcasted_iota(jnp.int32, (n, n), 1)
    eye = jnp.where(ri == ci, 1.0, 0.0)
    d = jnp.where(ri // INV_BLOCK == ci // INV_BLOCK, x, 0.0)
    off = x - d
    d2 = _dot3(d, d)
    d4 = _dot3(d2, d2)
    d8 = _dot3(d4, d4)
    p = eye - d
    p = p + _dot3(p, d2)
    p = p + _dot3(p, d4)
    dinv = p + _dot3(p, d8)
    m = _dot3(dinv, off)
    q = eye - m
    q = q + _dot3(q, _dot3(m, m))
    return _dot3(q, dinv)


def _scan_kernel(per_chunk_state, *refs):
    if per_chunk_state:
        (r_ref, k_ref, v_ref, kk_ref, b_ref, lw_ref, g_ref, bonus_ref, gn_ref, s0_ref,
         ya_ref, sout_ref, s_scr) = refs
    else:
        (r_ref, k_ref, v_ref, kk_ref, b_ref, lw_ref, g_ref, bonus_ref, gn_ref,
         ya_ref, sout_ref, s_scr) = refs
    c = pl.program_id(1)
    C = r_ref.shape[0]
    assert C == SCAN_C and SCAN_C // INV_BLOCK == 4 and 2 * C == PAIR

    if per_chunk_state:
        s_scr[...] = s0_ref[0, 0]
    else:
        @pl.when(c == 0)
        def _():
            s_scr[...] = jnp.zeros((PAIR, PAIR), f32)

    S = s_scr[...]
    r, k, v, kk, b, lw = (x[...] for x in (r_ref, k_ref, v_ref, kk_ref, b_ref, lw_ref))
    lane = lax.broadcasted_iota(jnp.int32, (1, PAIR), 1)
    lane_a = lane < HEAD_DIM
    t_idx = lax.broadcasted_iota(jnp.int32, (C, 1), 0)
    tri = jnp.where(lax.broadcasted_iota(jnp.int32, (C, C), 1) <= t_idx, 1.0, 0.0).astype(bf16)
    cs = _dot_exact_lhs(tri, lw)
    total = cs[C - 1:C, :]
    kp = kk * jnp.exp(cs - lw)
    rp = r * jnp.exp(cs)
    g_inv = jnp.exp(-cs)
    g_tail = jnp.exp(total - cs)
    kt, bt = k * g_inv, b * g_inv
    kh, bh = k * g_tail, b * g_tail

    st = lambda x: _stack_heads(x, lane_a)
    sc = _dot3(jnp.concatenate([kp, rp], axis=0),
               jnp.concatenate([st(bt), st(kt)], axis=0), NT)
    s_idx = lane % C
    strict = s_idx < t_idx
    incl = s_idx <= t_idx
    lb = jnp.where(strict, sc[:C, :PAIR], 0.0)
    lk = jnp.where(strict, sc[:C, PAIR:], 0.0)
    pb = jnp.where(incl, sc[C:, :PAIR], 0.0)
    pk = jnp.where(incl, sc[C:, PAIR:], 0.0)

    tinv = _unit_lower_inverse(st(lb))
    tcat = tinv[:C] + tinv[C:]
    lkv = _dot3(lk, st(v))
    w12 = _dot3(tcat, jnp.concatenate([st(lkv), st(kp)], axis=1))
    w1, w2 = w12[:, :PAIR], w12[:, PAIR:]
    pbw = _dot(pb.astype(bf16), jnp.concatenate([st(w1), st(w2)], axis=1).astype(bf16))
    pkv = _dot(pk.astype(bf16), st(v).astype(bf16))
    q2 = rp - pbw[:, PAIR:]
    y = _dot(q2.astype(bf16), S.astype(bf16), NT) + (pkv - pbw[:, :PAIR])

    ri = lax.broadcasted_iota(jnp.int32, (PAIR, PAIR), 0)
    ci = lax.broadcasted_iota(jnp.int32, (PAIR, PAIR), 1)
    same_head = (ri // HEAD_DIM) == (ci // HEAD_DIM)
    wb = _dot3(jnp.concatenate([w2, w1], axis=1), bh, TN)
    g2m = jnp.where(ri == ci, jnp.exp(total), 0.0) - jnp.where(same_head, wb[:PAIR], 0.0)
    g1m = jnp.where(same_head, _dot3(v, kh, TN) - wb[PAIR:], 0.0)
    s_new = _dot3(S, g2m) + g1m
    s_scr[...] = s_new
    sout_ref[0, 0] = s_new

    ones = jnp.where(same_head, 1.0, 0.0).astype(bf16)
    mean = _dot_exact_rhs(y, ones) * (1.0 / HEAD_DIM)
    yc = y - mean
    var = _dot_exact_rhs(yc * yc, ones) * (1.0 / HEAD_DIM)
    yn = yc * lax.rsqrt(var + GN_EPS) * gn_ref[0:1, :] + gn_ref[1:2, :]
    ya_ref[...] = (yn + bonus_ref[...]) * g_ref[...]


def _rwkv_scan(arrs, gn, n_rows_out, row_block_off, n_chunks, s0=None):
    per_chunk = s0 is not None
    blk = pl.BlockSpec((SCAN_C, PAIR), lambda p, c: (c + row_block_off, p))
    in_specs = [blk] * 8 + [pl.BlockSpec((2, PAIR), lambda p, c: (0, p))]
    args = list(arrs) + [gn]
    if per_chunk:
        in_specs.append(pl.BlockSpec((1, 1, PAIR, PAIR), lambda p, c: (c, p, 0, 0)))
        args.append(s0)
        n_states = n_chunks
        s_map = lambda p, c: (c, p, 0, 0)
    else:
        n_states = 1
        s_map = lambda p, c: (0, p, 0, 0)
    return pl.pallas_call(
        functools.partial(_scan_kernel, per_chunk),
        grid=(N_PAIR, n_chunks),
        in_specs=in_specs,
        out_specs=[blk, pl.BlockSpec((1, 1, PAIR, PAIR), s_map)],
        out_shape=[jax.ShapeDtypeStruct((n_rows_out, D_A), f32),
                   jax.ShapeDtypeStruct((n_states, N_PAIR, PAIR, PAIR), f32)],
        scratch_shapes=[pltpu.VMEM((PAIR, PAIR), f32)],
        compiler_params=_cparams("arbitrary", "arbitrary"),
        name="rwkv_scan",
    )(*args)


def _pair_states(state):
    b = state.shape[0]
    s = state.reshape(b, N_PAIR, 2, HEAD_DIM, HEAD_DIM)
    out = jnp.zeros((b, N_PAIR, 2, HEAD_DIM, 2, HEAD_DIM), f32)
    out = out.at[:, :, 0, :, 0, :].set(s[:, :, 0]).at[:, :, 1, :, 1, :].set(s[:, :, 1])
    return out.reshape(b, N_PAIR, PAIR, PAIR)


def _unpair_states(s):
    b = s.shape[0]
    s = s.reshape(b, N_PAIR, 2, HEAD_DIM, 2, HEAD_DIM)
    return jnp.stack([s[:, :, 0, :, 0, :], s[:, :, 1, :, 1, :]], axis=2).reshape(b, H_A, HEAD_DIM, HEAD_DIM)


def _rwkv_mixer(pa, state_shift, state_wkv, prm, j, n_sample, n_valid, dec_seq):
    n = pa.shape[0]
    db = n_sample // dec_seq
    ss_rows = jnp.zeros((db, dec_seq, A_IN), f32).at[:, 0].set(state_shift).reshape(n_sample, A_IN)
    vecs = jnp.zeros((8, D_A), f32)
    for idx, name in enumerate(("w0", "a0", "k_k", "k_a", "r_k")):
        vecs = vecs.at[idx].set(prm[name][j].reshape(D_A))
    zpad = jnp.zeros((LORA_W, D_A), f32)
    w2p = jnp.concatenate([prm["w2"][j], zpad], axis=0)
    a2p = jnp.concatenate([zpad, prm["a2"][j]], axis=0)
    arrs = _rwkv_prep(pa, ss_rows, prm["mu_shift"][j].reshape(1, A_IN), vecs, w2p, a2p, prm["g2"][j],
                      n_sample, n_valid, dec_seq)
    gn = jnp.stack([prm["gn_g"][j], prm["gn_b"][j]])
    n_chunks = -(-(n_valid - n_sample) // SCAN_C)
    assert n_sample % SCAN_C == 0 and n_sample + n_chunks * SCAN_C <= n and dec_seq <= SCAN_C
    ya_p, s_p = _rwkv_scan(arrs, gn, n, n_sample // SCAN_C, n_chunks)

    def pad_seq(x):
        x = x[:n_sample].reshape(db, dec_seq, D_A)
        return jnp.pad(x, ((0, 0), (0, SCAN_C - dec_seq), (0, 0))).reshape(db * SCAN_C, D_A)

    ya_s, s_s = _rwkv_scan([pad_seq(x) for x in arrs], gn, db * SCAN_C, 0, db, s0=_pair_states(state_wkv))
    ya_s = ya_s.reshape(db, SCAN_C, D_A)[:, :dec_seq].reshape(n_sample, D_A)
    ya = jnp.concatenate([ya_s, ya_p[n_sample:]], axis=0)
    return ya, _unpair_states(s_p[0][None]), _unpair_states(s_s)


CUM_BLOCK = 256


def _upper_ones(n):
    r = lax.broadcasted_iota(jnp.int32, (n, n), 0)
    c = lax.broadcasted_iota(jnp.int32, (n, n), 1)
    return jnp.where(r <= c, 1.0, 0.0).astype(bf16)


def _cumsum_kernel(x_ref, o_ref, carry):
    @pl.when(pl.program_id(0) == 0)
    def _():
        carry[...] = jnp.zeros_like(carry)

    x = x_ref[...]
    n = x.shape[1]
    o_ref[...] = _dot_exact_rhs(x, _upper_ones(n)) + carry[:, 0:1]
    carry[...] = carry[...] + _dot_exact_rhs(x, jnp.ones((n, LANES), bf16))


def _logf_cumsum(lf_t, first_block):
    h, n = lf_t.shape
    return pl.pallas_call(
        _cumsum_kernel,
        grid=(n // CUM_BLOCK - first_block,),
        in_specs=[pl.BlockSpec((h, CUM_BLOCK), lambda i: (0, i + first_block))],
        out_specs=pl.BlockSpec((h, CUM_BLOCK), lambda i: (0, i + first_block)),
        out_shape=jax.ShapeDtypeStruct((h, n), f32),
        scratch_shapes=[pltpu.VMEM((h, LANES), f32)],
        compiler_params=_cparams("arbitrary"),
        name="logf_cumsum",
    )(lf_t)


def _fox_prompt_kernel(row0, q_ref, k_ref, v_ref, c_ref, o_ref):
    i = pl.program_id(1)
    q = q_ref[...]
    lane_a = lax.broadcasted_iota(jnp.int32, (1, PAIR), 1) < HEAD_DIM
    rows = lax.broadcasted_iota(jnp.int32, (FOX_TQ, 1), 0)
    cols = lax.broadcasted_iota(jnp.int32, (1, FOX_TK), 1)
    zero = jnp.zeros_like(q)
    outs = []
    for hh in range(2):
        qh = jnp.where(lane_a, q, zero) if hh == 0 else jnp.where(lane_a, zero, q)

        def step(j, carry, masked, hh=hh, qh=qh):
            m, l, acc = carry
            start = pl.multiple_of(row0 + j * FOX_TK, FOX_TK)
            kb = k_ref[pl.ds(start, FOX_TK), :]
            vb = v_ref[pl.ds(start, FOX_TK), :]
            s = _dot(qh, kb, NT) - c_ref[0, hh:hh + 1, pl.ds(start, FOX_TK)]
            if masked:
                s = jnp.where(cols <= rows, s, NEG)
            m_new = jnp.maximum(m, jnp.max(s, axis=1, keepdims=True))
            alpha = jnp.exp(m - m_new)
            p = jnp.exp(s - m_new)
            l = l * alpha + jnp.sum(p, axis=1, keepdims=True)
            acc = acc * alpha + _dot(p.astype(bf16), vb)
            return m_new, l, acc

        init = (jnp.full((FOX_TQ, 1), NEG, f32), jnp.zeros((FOX_TQ, 1), f32), jnp.zeros((FOX_TQ, PAIR), f32))
        carry = lax.fori_loop(0, i, functools.partial(step, masked=False), init)
        m, l, acc = step(i, carry, True)
        outs.append(acc / l)
    o_ref[...] = jnp.where(lane_a, outs[0], outs[1])


def _fox_prompt(qkv16, c_pairs, n_sample, n_valid):
    n = qkv16.shape[0]
    assert FOX_TQ == FOX_TK and n_sample % FOX_TQ == 0
    nq = -(-(n_valid - n_sample) // FOX_TQ)
    assert n_sample + nq * FOX_TQ <= n
    qb0 = n_sample // FOX_TQ
    nb = D_B // PAIR
    return pl.pallas_call(
        functools.partial(_fox_prompt_kernel, n_sample),
        grid=(nb, nq),
        in_specs=[
            pl.BlockSpec((FOX_TQ, PAIR), lambda p, i: (i + qb0, p)),
            pl.BlockSpec((n, PAIR), lambda p, i: (0, nb + p)),
            pl.BlockSpec((n, PAIR), lambda p, i: (0, 2 * nb + p)),
            pl.BlockSpec((1, 2, n), lambda p, i: (p, 0, 0)),
        ],
        out_specs=pl.BlockSpec((FOX_TQ, PAIR), lambda p, i: (i + qb0, p)),
        out_shape=jax.ShapeDtypeStruct((n, D_B), f32),
        compiler_params=_cparams("parallel", "arbitrary"),
        name="fox_prompt",
    )(qkv16, qkv16, qkv16, c_pairs)


def _fox_sample_kernel(n_steps, page_size, pt_ref, q_ref, kn_ref, vn_ref, lfn_ref, *refs):
    np_ = PAGES_PER_STEP
    k_refs, v_refs, lf_refs = refs[:np_], refs[np_:2 * np_], refs[2 * np_:3 * np_]
    o_ref, m_scr, l_scr, acc_scr, carry_scr = refs[3 * np_:]
    j = pl.program_id(1)
    ds = q_ref.shape[1]
    nrow = H_B * ds

    @pl.when(j == 0)
    def _():
        m_scr[...] = jnp.full(m_scr.shape, NEG, f32)
        l_scr[...] = jnp.zeros(l_scr.shape, f32)
        acc_scr[...] = jnp.zeros(acc_scr.shape, f32)
        carry_scr[...] = jnp.zeros(carry_scr.shape, f32)

    row_h = lax.broadcasted_iota(jnp.int32, (nrow, D_B), 0) // ds
    lane_h = lax.broadcasted_iota(jnp.int32, (nrow, D_B), 1) // HEAD_DIM
    head_sel = row_h == lane_h
    q = q_ref[0]
    qx = jnp.where(head_sel, jnp.broadcast_to(q[None], (H_B, ds, D_B)).reshape(nrow, D_B), 0.0).astype(bf16)
    upper = _upper_ones(page_size)
    all_ones = jnp.ones((page_size, LANES), bf16)

    def attend(kp, vp, lfp, mask):
        s = _dot(qx, kp.astype(bf16), NT)
        cum = _dot_exact_rhs(lfp, upper) + carry_scr[:, 0:1]
        carry_scr[...] = carry_scr[...] + _dot_exact_rhs(lfp, all_ones)
        s = s - jnp.broadcast_to(cum[:, None, :], (H_B, ds, page_size)).reshape(nrow, page_size)
        if mask is not None:
            s = jnp.where(mask, s, NEG)
        m = m_scr[...]
        m_new = jnp.maximum(m, jnp.max(s, axis=1, keepdims=True))
        alpha = jnp.exp(m - m_new)
        p = jnp.exp(s - m_new)
        l_scr[...] = l_scr[...] * alpha + jnp.sum(p, axis=1, keepdims=True)
        acc_scr[...] = acc_scr[...] * alpha + _dot(p.astype(bf16), vp.astype(bf16))
        m_scr[...] = m_new

    for pp in range(np_):
        attend(k_refs[pp][0], v_refs[pp][0], lf_refs[pp][0], None)

    @pl.when(j == n_steps - 1)
    def _():
        pad = jnp.zeros((page_size - ds, D_B), f32)
        t_q = lax.broadcasted_iota(jnp.int32, (nrow, page_size), 0) % ds
        t_k = lax.broadcasted_iota(jnp.int32, (nrow, page_size), 1)
        attend(jnp.concatenate([kn_ref[...], pad], axis=0), jnp.concatenate([vn_ref[...], pad], axis=0),
               lfn_ref[0], t_k <= t_q)
        o = jnp.where(head_sel, acc_scr[...] / l_scr[...], 0.0)
        out = o[0:ds]
        for h in range(1, H_B):
            out = out + o[h * ds:(h + 1) * ds]
        o_ref[...] = out


def _fox_sample(q_s, kv, lfn_t, cache_k, cache_v, cache_lf_t, page_table, dec_seq):
    db = q_s.shape[0]
    n_pages = page_table.shape[1]
    page_size = cache_k.shape[1]
    assert n_pages % PAGES_PER_STEP == 0 and page_size == LANES
    n_steps = n_pages // PAGES_PER_STEP

    def page_map(pp):
        return lambda b, j, pt: (pt[b, j * PAGES_PER_STEP + pp], 0, 0)

    in_specs = [
        pl.BlockSpec((1, dec_seq, D_B), lambda b, j, pt: (b, 0, 0)),
        pl.BlockSpec((dec_seq, D_B), lambda b, j, pt: (b, 0)),
        pl.BlockSpec((dec_seq, D_B), lambda b, j, pt: (b, 1)),
        pl.BlockSpec((1, H_B, page_size), lambda b, j, pt: (b, 0, 0)),
    ]
    in_specs += [pl.BlockSpec((1, page_size, D_B), page_map(pp)) for pp in range(PAGES_PER_STEP)]
    in_specs += [pl.BlockSpec((1, page_size, D_B), page_map(pp)) for pp in range(PAGES_PER_STEP)]
    in_specs += [pl.BlockSpec((1, H_B, page_size), page_map(pp)) for pp in range(PAGES_PER_STEP)]
    nrow = H_B * dec_seq
    grid_spec = pltpu.PrefetchScalarGridSpec(
        num_scalar_prefetch=1,
        grid=(db, n_steps),
        in_specs=in_specs,
        out_specs=pl.BlockSpec((dec_seq, D_B), lambda b, j, pt: (b, 0)),
        scratch_shapes=[pltpu.VMEM((nrow, 1), f32), pltpu.VMEM((nrow, 1), f32),
                        pltpu.VMEM((nrow, D_B), f32), pltpu.VMEM((H_B, LANES), f32)],
    )
    return pl.pallas_call(
        functools.partial(_fox_sample_kernel, n_steps, page_size),
        grid_spec=grid_spec,
        out_shape=jax.ShapeDtypeStruct((db * dec_seq, D_B), f32),
        compiler_params=_cparams("parallel", "arbitrary"),
        name="fox_sample",
    )(page_table, q_s, kv, kv, lfn_t, *([cache_k] * PAGES_PER_STEP), *([cache_v] * PAGES_PER_STEP),
      *([cache_lf_t] * PAGES_PER_STEP))


def _fox_attention(qkv16, kv, lf, cache_k, cache_v, cache_lf, page_table, n_sample, n_valid, dec_seq):
    n = qkv16.shape[0]
    db = n_sample // dec_seq
    lf_t = lf[:, :H_B].T
    c = _logf_cumsum(lf_t, n_sample // CUM_BLOCK)
    yb_p = _fox_prompt(qkv16, c.reshape(D_B // PAIR, 2, n), n_sample, n_valid)
    n_pool, page_size = cache_k.shape[0], cache_k.shape[1]
    q_s = qkv16[:n_sample, :D_B].astype(f32).reshape(db, dec_seq, D_B)
    lfn_t = jnp.swapaxes(lf[:n_sample, :H_B].reshape(db, dec_seq, H_B), 1, 2)
    lfn_t = jnp.pad(lfn_t, ((0, 0), (0, 0), (0, page_size - dec_seq)))
    yb_s = _fox_sample(q_s, kv, lfn_t,
                       cache_k.reshape(n_pool, page_size, D_B), cache_v.reshape(n_pool, page_size, D_B),
                       jnp.swapaxes(cache_lf, 1, 2), page_table, dec_seq)
    return jnp.concatenate([yb_s, yb_p[n_sample:]], axis=0)


def _outproj_kernel(alpha, ya_ref, yb_ref, h_ref, w_ref, ln_ref, o_ref):
    y = _dot(ya_ref[...].astype(bf16), w_ref[:D_A, :]) + _dot(yb_ref[...].astype(bf16), w_ref[D_A:, :])
    o_ref[...] = _layer_norm(alpha * h_ref[...] + y, ln_ref[0:1, :], ln_ref[1:2, :])


def _out_projection(ya, yb, h, w_bf, ln, alpha):
    n = h.shape[0]
    half = pl.BlockSpec((ROW_TILE, D_A), lambda i: (i, 0))
    full = pl.BlockSpec((ROW_TILE, D_MODEL), lambda i: (i, 0))
    return pl.pallas_call(
        functools.partial(_outproj_kernel, alpha),
        grid=(n // ROW_TILE,),
        in_specs=[half, half, full,
                  pl.BlockSpec((D_MODEL, D_MODEL), lambda i: (0, 0)),
                  pl.BlockSpec((2, D_MODEL), lambda i: (0, 0))],
        out_specs=full,
        out_shape=jax.ShapeDtypeStruct((n, D_MODEL), f32),
        compiler_params=_cparams("parallel"),
        name="out_projection",
    )(ya, yb, h, w_bf, ln)


POOL_HIST = POOL_BUF + 1


def _pool_kernel(prompt_mode, zero_hist_tile, row0, alpha, x_ref, hist_ref, w_ref, sc_ref, ln_ref, o_ref):
    i = pl.program_id(0) + zero_hist_tile
    tm = x_ref.shape[0]
    x = x_ref[...]
    hist = jnp.where(i == zero_hist_tile, 0.0, hist_ref[...])
    ext = jnp.concatenate([hist, x], axis=0)
    t = i * tm + lax.broadcasted_iota(jnp.int32, (tm, 1), 0) - row0
    ys = []
    for gi, win in enumerate(POOL_WINDOWS):
        s = ext[:, gi * POOL_G:(gi + 1) * POOL_G]
        span = 1
        while span < win:
            s = s + pltpu.roll(s, span, 0)
            span *= 2
        cnt = jnp.minimum(win, t + 1).astype(f32) if prompt_mode else float(win)
        diff = s[POOL_HIST:] / cnt - x[:, gi * POOL_G:(gi + 1) * POOL_G]
        ys.append(_dot(diff.astype(bf16), w_ref[gi]))
    y = jnp.concatenate(ys, axis=1) * sc_ref[...]
    o_ref[...] = _layer_norm(alpha * x + y, ln_ref[0:1, :], ln_ref[1:2, :])


def _pool_mixer(x, w_bf, scale, ln, alpha, tm, first_tile, n_tiles, prompt_mode, row0):
    n = x.shape[0]
    assert tm % POOL_HIST == 0
    hb = tm // POOL_HIST
    return pl.pallas_call(
        functools.partial(_pool_kernel, prompt_mode, first_tile, row0, alpha),
        grid=(n_tiles,),
        in_specs=[
            pl.BlockSpec((tm, D_MODEL), lambda i: (i + first_tile, 0)),
            pl.BlockSpec((POOL_HIST, D_MODEL), lambda i: (jnp.maximum((i + first_tile) * hb - 1, 0), 0)),
            pl.BlockSpec((len(POOL_WINDOWS), POOL_G, POOL_G), lambda i: (0, 0, 0)),
            pl.BlockSpec((1, D_MODEL), lambda i: (0, 0)),
            pl.BlockSpec((2, D_MODEL), lambda i: (0, 0)),
        ],
        out_specs=pl.BlockSpec((tm, D_MODEL), lambda i: (i + first_tile, 0)),
        out_shape=jax.ShapeDtypeStruct((n, D_MODEL), f32),
        compiler_params=_cparams("parallel"),
        name="pool_mixer",
    )(x, x, w_bf, scale, ln)


def _peer_query_kernel(x_ref, wh_ref, wl_ref, o_ref):
    xh, xl = _split(x_ref[...])
    o_ref[...] = _dot(xh, wh_ref[...]) + (_dot(xl, wh_ref[...]) + _dot(xh, wl_ref[...]))


def _peer_query(x, wq_hi, wq_lo):
    n = x.shape[0]
    nq = wq_hi.shape[1]
    return pl.pallas_call(
        _peer_query_kernel,
        grid=(n // ROW_TILE,),
        in_specs=[pl.BlockSpec((ROW_TILE, D_MODEL), lambda i: (i, 0)),
                  pl.BlockSpec((D_MODEL, nq), lambda i: (0, 0)),
                  pl.BlockSpec((D_MODEL, nq), lambda i: (0, 0))],
        out_specs=pl.BlockSpec((ROW_TILE, nq), lambda i: (i, 0)),
        out_shape=jax.ShapeDtypeStruct((n, nq), f32),
        compiler_params=_cparams("parallel"),
        name="peer_query",
    )(x, wq_hi, wq_lo)


def _top_values(s, n):
    vals = []
    for _ in range(n):
        m = jnp.max(s, axis=0, keepdims=True)
        vals.append(m)
        s = jnp.where(s == m, -jnp.inf, s)
    return jnp.concatenate(vals, axis=0)


def _peer_router(s1, s2):
    v1 = _top_values(s1, PEER_TOPK)
    v2 = _top_values(s2, PEER_TOPK)
    n_take = PEER_TOPK + 1
    cands = [v1[0:1] + v2]
    sub = lax.broadcasted_iota(jnp.int32, (8, 1), 0)
    for a in range(1, PEER_TOPK):
        lim = n_take // (a + 1)
        assert lim <= 8
        ca = v1[a:a + 1] + v2[0:8]
        cands.append(ca if lim == 8 else jnp.where(sub < lim, ca, -jnp.inf))
    cand = jnp.concatenate(cands, axis=0)
    top = _top_values(cand, n_take)
    tau = 0.5 * (top[PEER_TOPK - 1:PEER_TOPK] + top[PEER_TOPK:PEER_TOPK + 1])
    z = jnp.sum(jnp.where(cand > tau, jnp.exp(cand - top[0:1]), 0.0), axis=0, keepdims=True)
    last = PEER_TOPK - 1
    e1 = jnp.where(s1 >= v1[last:last + 1], jnp.exp(s1 - v1[0:1]) / z, 0.0)
    p = jnp.where(s2 >= v2[last:last + 1], jnp.exp(s2 - v2[0:1]), 0.0)
    return e1, tau - s1, p


INV_SQRT2 = 1.0 / math.sqrt(2.0)


def _peer_kernel(alpha, n_chunks, x_ref, q_ref, keys_ref, u_ref, vt_ref, ln_ref, o_ref,
                 xt_scr, s2_scr, p_scr, e1_scr, c_scr, acc_scr):
    j = pl.program_id(1)
    tn = x_ref.shape[0]

    @pl.when(j == 0)
    def _():
        xt_scr[...] = x_ref[...].T.astype(bf16)
        acc_scr[...] = jnp.zeros(acc_scr.shape, f32)

        def route(h, carry):
            col = pl.multiple_of(h * 2 * PEER_HALF, 2 * PEER_HALF)
            q1 = q_ref[:, pl.ds(col, PEER_HALF)]
            q2 = q_ref[:, pl.ds(col + PEER_HALF, PEER_HALF)]
            s1 = _dot3(keys_ref[2 * h], q1, NT)
            s2 = _dot3(keys_ref[2 * h + 1], q2, NT)
            e1, c, p = _peer_router(s1, s2)
            s2_scr[h] = s2
            p_scr[h] = p
            e1_scr[h] = e1
            c_scr[h] = c
            return carry

        lax.fori_loop(0, PEER_HEADS, route, 0)

    ht = _dot(u_ref[...], xt_scr[...])
    n_rows = PEER_TE // PEER_NKEYS
    parts = []
    for rr in range(n_rows):
        r = j * n_rows + rr
        w = jnp.zeros((PEER_NKEYS, tn), f32)
        for h in range(PEER_HEADS):
            sel = s2_scr[h] >= c_scr[h, pl.ds(r, 1), :]
            w = w + jnp.where(sel, p_scr[h], 0.0) * e1_scr[h, pl.ds(r, 1), :]
        hh = ht[rr * PEER_NKEYS:(rr + 1) * PEER_NKEYS]
        parts.append((w * (0.5 * hh * (1.0 + lax.erf(hh * INV_SQRT2)))).astype(bf16))
    acc_scr[...] += _dot(vt_ref[...], jnp.concatenate(parts, axis=0))

    @pl.when(j == n_chunks - 1)
    def _():
        y = acc_scr[...].T
        o_ref[...] = _layer_norm(alpha * x_ref[...] + y, ln_ref[0:1, :], ln_ref[1:2, :])


def _peer_ffn(x, wq_hi, wq_lo, keys, u_bf, vt_bf, ln, alpha):
    n = x.shape[0]
    n_exp = u_bf.shape[0]
    assert n % PEER_TN == 0 and n_exp == PEER_NKEYS * PEER_NKEYS and n_exp % PEER_TE == 0
    n_chunks = n_exp // PEER_TE
    q = _peer_query(x, wq_hi, wq_lo)
    nq = q.shape[1]
    head_scr = pltpu.VMEM((PEER_HEADS, PEER_NKEYS, PEER_TN), f32)
    return pl.pallas_call(
        functools.partial(_peer_kernel, alpha, n_chunks),
        grid=(n // PEER_TN, n_chunks),
        in_specs=[
            pl.BlockSpec((PEER_TN, D_MODEL), lambda i, j: (i, 0)),
            pl.BlockSpec((PEER_TN, nq), lambda i, j: (i, 0)),
            pl.BlockSpec((2 * PEER_HEADS, PEER_NKEYS, PEER_HALF), lambda i, j: (0, 0, 0)),
            pl.BlockSpec((PEER_TE, D_MODEL), lambda i, j: (j, 0)),
            pl.BlockSpec((D_MODEL, PEER_TE), lambda i, j: (0, j)),
            pl.BlockSpec((2, D_MODEL), lambda i, j: (0, 0)),
        ],
        out_specs=pl.BlockSpec((PEER_TN, D_MODEL), lambda i, j: (i, 0)),
        out_shape=jax.ShapeDtypeStruct((n, D_MODEL), f32),
        scratch_shapes=[pltpu.VMEM((D_MODEL, PEER_TN), bf16), head_scr, head_scr, head_scr, head_scr,
                        pltpu.VMEM((D_MODEL, PEER_TN), f32)],
        compiler_params=_cparams("parallel", "arbitrary"),
        name="peer_ffn",
    )(x, q, keys, u_bf, vt_bf, ln)


POOL_TM = 256


def kernel(x_prompt, x_sample, cache_k, cache_v, cache_logf, page_table, state_wkv, state_shift, state_pool,
           meta, w_in, b_f, w_o, mu_shift, w0, w2, a0, a2, g2, k_k, k_a, r_k, gn_g, gn_b, w_pool, pool_scale,
           ln_g, ln_b, peer_wq, peer_keys, peer_u, peer_v):
    prm = dict(mu_shift=mu_shift, w0=w0, w2=w2, a0=a0, a2=a2, g2=g2, k_k=k_k, k_a=k_a, r_k=r_k,
               gn_g=gn_g, gn_b=gn_b)
    bp, seq, d = x_prompt.shape
    db, dec_seq, _ = x_sample.shape
    assert bp == 1 and d == D_MODEL
    depth = ln_g.shape[0]
    alpha = (2 * depth) ** 0.25
    n_sample = db * dec_seq
    t_p = seq + N_META
    n_valid = n_sample + t_p
    n_rows = n_sample + -(-t_p // FOX_TQ) * FOX_TQ
    n_rows = -(-n_rows // ROW_TILE) * ROW_TILE
    h = jnp.concatenate([x_sample.reshape(n_sample, d), meta.astype(f32), x_prompt[0],
                         jnp.zeros((n_rows - n_valid, d), f32)], axis=0)

    ks, vs, lfs, wkvs, shifts, pools = ([[], []] for _ in range(6))
    n_qkv = A_IN + 3 * D_B
    for i in range(depth):
        j = i // 2
        ln1 = jnp.stack([ln_g[i, 0], ln_b[i, 0]])
        ln2 = jnp.stack([ln_g[i, 1], ln_b[i, 1]])
        if i % 2 == 0:
            w_cat = jnp.concatenate([w_in[j][:, :n_qkv],
                                     jnp.pad(w_in[j][:, n_qkv:], ((0, 0), (0, LANES - H_B)))], axis=1).astype(bf16)
            b_pad = jnp.pad(b_f[j], (0, LANES - H_B)).reshape(1, LANES)
            pa, kv, qkv16, lf = _in_projection(h, w_cat, b_pad)
            ya, wkv_p, wkv_s = _rwkv_mixer(pa, state_shift[j], state_wkv[j], prm, j, n_sample, n_valid, dec_seq)
            yb = _fox_attention(qkv16, kv, lf, cache_k[j], cache_v[j], cache_logf[j], page_table,
                                n_sample, n_valid, dec_seq)
            h1 = _out_projection(ya, yb, h, w_o[j].astype(bf16), ln1, alpha)
            for grp, lo, hi, lead in ((0, n_sample, n_valid, (bp, t_p)), (1, 0, n_sample, (db, dec_seq))):
                ks[grp].append(kv[lo:hi, :D_B].reshape(lead + (H_B, HEAD_DIM)))
                vs[grp].append(kv[lo:hi, D_B:].reshape(lead + (H_B, HEAD_DIM)))
                lfs[grp].append(lf[lo:hi, :H_B].reshape(lead + (H_B,)))
                shifts[grp].append(pa[lo:hi].reshape(lead + (A_IN,))[:, -1])
            wkvs[0].append(wkv_p)
            wkvs[1].append(wkv_s)
        else:
            ext_s = jnp.concatenate([jnp.zeros((db, POOL_HIST - POOL_BUF, d), f32), state_pool[j],
                                     h[:n_sample].reshape(db, dec_seq, d)], axis=1)
            pools[0].append(h[n_valid - POOL_BUF:n_valid][None])
            pools[1].append(ext_s[:, -POOL_BUF:])
            w_bf = w_pool[j].astype(bf16)
            scale = pool_scale[j].reshape(1, d)
            first = n_sample // POOL_TM
            hp = _pool_mixer(h, w_bf, scale, ln1, alpha, POOL_TM, first, n_rows // POOL_TM - first, True, n_sample)
            ext_len = ext_s.shape[1]
            hs = _pool_mixer(ext_s.reshape(db * ext_len, d), w_bf, scale, ln1, alpha, db * ext_len, 0, 1, False, 0)
            hs = hs.reshape(db, ext_len, d)[:, POOL_HIST:].reshape(n_sample, d)
            h1 = jnp.concatenate([hs, hp[n_sample:]], axis=0)
        wq = peer_wq[i]
        wq_hi = wq.astype(bf16)
        wq_lo = (wq - wq_hi.astype(f32)).astype(bf16)
        h = _peer_ffn(h1, wq_hi, wq_lo, peer_keys[i].reshape(2 * PEER_HEADS, PEER_NKEYS, PEER_HALF),
                      peer_u[i].astype(bf16), peer_v[i].T.astype(bf16), ln2, alpha)

    y_prompt = h[n_sample + N_META:n_valid][None]
    y_sample = h[:n_sample].reshape(db, dec_seq, d)
    st = lambda xs: jnp.stack(xs)
    return (y_prompt, y_sample,
            st(ks[0]), st(vs[0]), st(lfs[0]), st(wkvs[0]), st(shifts[0]), st(pools[0]),
            st(ks[1]), st(vs[1]), st(lfs[1]), st(wkvs[1]), st(shifts[1]), st(pools[1]))
```

```python
import functools
import math

import jax
import jax.numpy as jnp
from jax import lax
from jax.experimental import pallas as pl
from jax.experimental.pallas import tpu as pltpu

f32 = jnp.float32
bf16 = jnp.bfloat16

D_MODEL = 1024
N_META = 16
HEAD_DIM = 64
D_A = D_MODEL // 2
D_B = D_MODEL - D_A
H_A = D_A // HEAD_DIM
H_B = D_B // HEAD_DIM
LORA_W = 64
LORA_A = 64
LORA_G = 128
A_K = D_A
A_V = 2 * D_A
A_W = 3 * D_A
A_IN = A_W + LORA_W + LORA_A + LORA_G
GN_EPS = HEAD_DIM * 1e-5
POOL_WINDOWS = (2, 4, 8, 16)
POOL_G = D_MODEL // len(POOL_WINDOWS)
POOL_BUF = max(POOL_WINDOWS) - 1
PEER_HEADS = 8
PEER_NKEYS = 128
PEER_TOPK = 16
PEER_HALF = 128
LN_EPS = 1e-5
NEG = -1e30

LANES = 128
PAIR = 2 * HEAD_DIM
N_PAIR = D_A // PAIR
ROW_TILE = 512
SCAN_C = 64
INV_BLOCK = 16
FOX_TQ = 256
FOX_TK = 256
PAGES_PER_STEP = 8
PEER_TN = 256
PEER_TE = 512
VMEM_LIMIT = 56 * 1024 * 1024


def _cparams(*sem):
    return pltpu.CompilerParams(dimension_semantics=sem, vmem_limit_bytes=VMEM_LIMIT)


def _dot(a, b, dims=(((1,), (0,)), ((), ()))):
    return lax.dot_general(a, b, dims, preferred_element_type=f32)


NT = (((1,), (1,)), ((), ()))
TN = (((0,), (0,)), ((), ()))
NN = (((1,), (0,)), ((), ()))


def _split(x):
    hi = x.astype(bf16)
    lo = (x - hi.astype(f32)).astype(bf16)
    return hi, lo


def _dot3(a, b, dims=NN):
    ah, al = _split(a)
    bh, bl = _split(b)
    return _dot(ah, bh, dims) + (_dot(al, bh, dims) + _dot(ah, bl, dims))


def _dot_exact_rhs(a, b_bf, dims=NN):
    a0 = a.astype(bf16)
    r1 = a - a0.astype(f32)
    a1 = r1.astype(bf16)
    a2 = (r1 - a1.astype(f32)).astype(bf16)
    return _dot(a0, b_bf, dims) + (_dot(a1, b_bf, dims) + _dot(a2, b_bf, dims))


def _dot_exact_lhs(a_bf, b, dims=NN):
    b0 = b.astype(bf16)
    r1 = b - b0.astype(f32)
    b1 = r1.astype(bf16)
    b2 = (r1 - b1.astype(f32)).astype(bf16)
    return _dot(a_bf, b0, dims) + (_dot(a_bf, b1, dims) + _dot(a_bf, b2, dims))


def _layer_norm(x, g, b):
    mu = jnp.mean(x, axis=-1, keepdims=True)
    xc = x - mu
    var = jnp.mean(xc * xc, axis=-1, keepdims=True)
    return xc * lax.rsqrt(var + LN_EPS) * g + b


def _head_ones():
    r = lax.broadcasted_iota(jnp.int32, (D_A, D_A), 0) // HEAD_DIM
    c = lax.broadcasted_iota(jnp.int32, (D_A, D_A), 1) // HEAD_DIM
    return jnp.where(r == c, 1.0, 0.0).astype(bf16)


def _proj_kernel(x_ref, w_ref, bf_ref, pa_ref, kv_ref, qkv16_ref, lf_ref):
    x = x_ref[...].astype(bf16)
    y = _dot(x, w_ref[...])
    pa_ref[...] = y[:, :A_IN]
    q = y[:, A_IN:A_IN + D_B]
    kv = y[:, A_IN + D_B:A_IN + 3 * D_B]
    kv_ref[...] = kv
    qkv16_ref[:, :D_B] = (q * (HEAD_DIM ** -0.5)).astype(bf16)
    qkv16_ref[:, D_B:] = kv.astype(bf16)
    lf_ref[...] = jax.nn.log_sigmoid(y[:, A_IN + 3 * D_B:] + bf_ref[...])


def _in_projection(x, w_bf, b_f_pad):
    n = x.shape[0]
    nw = w_bf.shape[1]
    return pl.pallas_call(
        _proj_kernel,
        grid=(n // ROW_TILE,),
        in_specs=[
            pl.BlockSpec((ROW_TILE, D_MODEL), lambda i: (i, 0)),
            pl.BlockSpec((D_MODEL, nw), lambda i: (0, 0)),
            pl.BlockSpec((1, LANES), lambda i: (0, 0)),
        ],
        out_specs=[
            pl.BlockSpec((ROW_TILE, A_IN), lambda i: (i, 0)),
            pl.BlockSpec((ROW_TILE, 2 * D_B), lambda i: (i, 0)),
            pl.BlockSpec((ROW_TILE, 3 * D_B), lambda i: (i, 0)),
            pl.BlockSpec((ROW_TILE, LANES), lambda i: (i, 0)),
        ],
        out_shape=[
            jax.ShapeDtypeStruct((n, A_IN), f32),
            jax.ShapeDtypeStruct((n, 2 * D_B), f32),
            jax.ShapeDtypeStruct((n, 3 * D_B), bf16),
            jax.ShapeDtypeStruct((n, LANES), f32),
        ],
        compiler_params=_cparams("parallel"),
        name="in_projection",
    )(x, w_bf, b_f_pad)


PREP_TM = 256


def _prep_kernel(n_sample, n_valid, dec_seq,
                 pa_ref, prev_ref, ss_ref, mu_ref, vec_ref, w2_ref, a2_ref, g2_ref,
                 r_o, k_o, v_o, kk_o, b_o, lw_o, g_o, bonus_o):
    i = pl.program_id(0)
    tm = pa_ref.shape[0]
    pa = pa_ref[...]
    local = lax.broadcasted_iota(jnp.int32, (tm, 1), 0)
    row = i * tm + local
    prev = pltpu.roll(pa, 1, 0)
    prev = jnp.where(local == 0, prev_ref[7:8, :], prev)
    prev = jnp.where(row == n_sample, 0.0, prev)
    seq_start = jnp.logical_and(row < n_sample, row % dec_seq == 0)
    prev = jnp.where(seq_start, ss_ref[...], prev)
    xs = pa + (prev - pa) * mu_ref[...]

    w0 = vec_ref[0:1, :]
    a0 = vec_ref[1:2, :]
    k_k = vec_ref[2:3, :]
    k_a = vec_ref[3:4, :]
    r_k = vec_ref[4:5, :]
    r = xs[:, :A_K]
    k = xs[:, A_K:A_V]
    v = xs[:, A_V:A_W]
    wa = xs[:, A_W:A_W + LORA_W + LORA_A]
    xg = xs[:, A_W + LORA_W + LORA_A:]
    w = -jax.nn.softplus(-(w0 + _dot3(jnp.tanh(wa), w2_ref[...]))) - 0.5
    lw = -jnp.exp(w)
    a = jax.nn.sigmoid(a0 + _dot3(wa, a2_ref[...]))
    g = _dot(jax.nn.sigmoid(xg).astype(bf16), g2_ref[...].astype(bf16))
    ones = _head_ones()
    kk = k * k_k
    kk = kk / jnp.maximum(jnp.sqrt(_dot_exact_rhs(kk * kk, ones)), 1e-12)
    k = k * (1.0 + (a - 1.0) * k_a)
    bonus = _dot_exact_rhs(r * k * r_k, ones) * v
    valid = row < n_valid
    r_o[...] = r
    k_o[...] = jnp.where(valid, k, 0.0)
    v_o[...] = jnp.where(valid, v, 0.0)
    kk_o[...] = kk
    b_o[...] = jnp.where(valid, kk * a, 0.0)
    lw_o[...] = jnp.where(valid, lw, 0.0)
    g_o[...] = g
    bonus_o[...] = bonus


def _rwkv_prep(pa, ss_rows, mu, vecs, w2p, a2p, g2, n_sample, n_valid, dec_seq):
    n = pa.shape[0]
    assert n_sample == PREP_TM and n % PREP_TM == 0
    row_spec = pl.BlockSpec((PREP_TM, D_A), lambda i: (i, 0))
    const = lambda shape: pl.BlockSpec(shape, lambda i: (0, 0))
    return pl.pallas_call(
        functools.partial(_prep_kernel, n_sample, n_valid, dec_seq),
        grid=(n // PREP_TM,),
        in_specs=[
            pl.BlockSpec((PREP_TM, A_IN), lambda i: (i, 0)),
            pl.BlockSpec((8, A_IN), lambda i: (jnp.maximum(i * (PREP_TM // 8) - 1, 0), 0)),
            const((PREP_TM, A_IN)),
            const((1, A_IN)),
            const((8, D_A)),
            const((LORA_W + LORA_A, D_A)),
            const((LORA_W + LORA_A, D_A)),
            const((LORA_G, D_A)),
        ],
        out_specs=[row_spec] * 8,
        out_shape=[jax.ShapeDtypeStruct((n, D_A), f32)] * 8,
        compiler_params=_cparams("parallel"),
        name="rwkv_prep",
    )(pa, pa, ss_rows, mu, vecs, w2p, a2p, g2)


def _stack_heads(x, lane_a):
    return jnp.concatenate([jnp.where(lane_a, x, 0.0), jnp.where(lane_a, 0.0, x)], axis=0)


def _unit_lower_inverse(x):
    n = x.shape[0]
    ri = lax.broadcasted_iota(jnp.int32, (n, n), 0)
    ci = lax.broadcasted_iota(jnp.int32, (n, n), 1)
    eye = jnp.where(ri == ci, 1.0, 0.0)
    d = jnp.where(ri // INV_BLOCK == ci // INV_BLOCK, x, 0.0)
    off = x - d
    d2 = _dot3(d, d)
    yield
    d4 = _dot3(d2, d2)
    yield
    d8 = _dot3(d4, d4)
    p = eye - d
    p = p + _dot3(p, d2)
    yield
    p = p + _dot3(p, d4)
    yield
    dinv = p + _dot3(p, d8)
    yield
    m = _dot3(dinv, off)
    yield
    m2 = _dot3(m, m)
    yield
    q = eye - m
    q = q + _dot3(q, m2)
    yield
    return _dot3(q, dinv)


def _scan_pair(r, k, v, kk, b, lw, g, bonus, gn, S):
    C = r.shape[0]
    assert C == SCAN_C and SCAN_C // INV_BLOCK == 4 and 2 * C == PAIR
    lane = lax.broadcasted_iota(jnp.int32, (1, PAIR), 1)
    lane_a = lane < HEAD_DIM
    t_idx = lax.broadcasted_iota(jnp.int32, (C, 1), 0)
    tri = jnp.where(lax.broadcasted_iota(jnp.int32, (C, C), 1) <= t_idx, 1.0, 0.0).astype(bf16)
    cs = _dot_exact_lhs(tri, lw)
    total = cs[C - 1:C, :]
    kp = kk * jnp.exp(cs - lw)
    rp = r * jnp.exp(cs)
    g_inv = jnp.exp(-cs)
    g_tail = jnp.exp(total - cs)
    kt, bt = k * g_inv, b * g_inv
    kh, bh = k * g_tail, b * g_tail

    st = lambda x: _stack_heads(x, lane_a)
    sc = _dot3(jnp.concatenate([kp, rp], axis=0),
               jnp.concatenate([st(bt), st(kt)], axis=0), NT)
    yield
    s_idx = lane % C
    strict = s_idx < t_idx
    incl = s_idx <= t_idx
    lb = jnp.where(strict, sc[:C, :PAIR], 0.0)
    lk = jnp.where(strict, sc[:C, PAIR:], 0.0)
    pb = jnp.where(incl, sc[C:, :PAIR], 0.0)
    pk = jnp.where(incl, sc[C:, PAIR:], 0.0)

    lkv = _dot3(lk, st(v))
    yield
    tinv = yield from _unit_lower_inverse(st(lb))
    yield
    tcat = tinv[:C] + tinv[C:]
    w12 = _dot3(tcat, jnp.concatenate([st(lkv), st(kp)], axis=1))
    yield
    w1, w2 = w12[:, :PAIR], w12[:, PAIR:]
    pbw = _dot(pb.astype(bf16), jnp.concatenate([st(w1), st(w2)], axis=1).astype(bf16))
    pkv = _dot(pk.astype(bf16), st(v).astype(bf16))
    yield
    q2 = rp - pbw[:, PAIR:]
    y = _dot(q2.astype(bf16), S.astype(bf16), NT) + (pkv - pbw[:, :PAIR])

    ri = lax.broadcasted_iota(jnp.int32, (PAIR, PAIR), 0)
    ci = lax.broadcasted_iota(jnp.int32, (PAIR, PAIR), 1)
    same_head = (ri // HEAD_DIM) == (ci // HEAD_DIM)
    wb = _dot3(jnp.concatenate([w2, w1], axis=1), bh, TN)
    yield
    g2m = jnp.where(ri == ci, jnp.exp(total), 0.0) - jnp.where(same_head, wb[:PAIR], 0.0)
    g1m = jnp.where(same_head, _dot3(v, kh, TN) - wb[PAIR:], 0.0)
    yield
    s_new = _dot3(S, g2m) + g1m
    yield

    ones = jnp.where(same_head, 1.0, 0.0).astype(bf16)
    mean = _dot_exact_rhs(y, ones) * (1.0 / HEAD_DIM)
    yc = y - mean
    var = _dot_exact_rhs(yc * yc, ones) * (1.0 / HEAD_DIM)
    yn = yc * lax.rsqrt(var + GN_EPS) * gn[0:1, :] + gn[1:2, :]
    return (yn + bonus) * g, s_new


def _scan_kernel(per_chunk_state, *refs):
    if per_chunk_state:
        (r_ref, k_ref, v_ref, kk_ref, b_ref, lw_ref, g_ref, bonus_ref, gn_ref, s0_ref,
         ya_ref, sout_ref, s_scr) = refs
        s_scr[...] = s0_ref[0]
    else:
        (r_ref, k_ref, v_ref, kk_ref, b_ref, lw_ref, g_ref, bonus_ref, gn_ref,
         ya_ref, sout_ref, s_scr) = refs

        @pl.when(pl.program_id(0) == 0)
        def _():
            s_scr[...] = jnp.zeros(s_scr.shape, f32)

    in_refs = (r_ref, k_ref, v_ref, kk_ref, b_ref, lw_ref, g_ref, bonus_ref, gn_ref)
    lanes = [slice(p * PAIR, (p + 1) * PAIR) for p in range(N_PAIR)]
    chains = [_scan_pair(*(x[:, sl] for x in in_refs), s_scr[p]) for p, sl in enumerate(lanes)]
    pending = list(range(N_PAIR))
    while pending:
        for p in list(pending):
            try:
                next(chains[p])
            except StopIteration as done:
                ya, s_new = done.value
                ya_ref[:, lanes[p]] = ya
                s_scr[p] = s_new
                sout_ref[0, p] = s_new
                pending.remove(p)


def _rwkv_scan(arrs, gn, n_rows_out, row_block_off, n_chunks, s0=None):
    per_chunk = s0 is not None
    blk = pl.BlockSpec((SCAN_C, D_A), lambda c: (c + row_block_off, 0))
    in_specs = [blk] * 8 + [pl.BlockSpec((2, D_A), lambda c: (0, 0))]
    args = list(arrs) + [gn]
    if per_chunk:
        in_specs.append(pl.BlockSpec((1, N_PAIR, PAIR, PAIR), lambda c: (c, 0, 0, 0)))
        args.append(s0)
        n_states = n_chunks
        s_map = lambda c: (c, 0, 0, 0)
    else:
        n_states = 1
        s_map = lambda c: (0, 0, 0, 0)
    return pl.pallas_call(
        functools.partial(_scan_kernel, per_chunk),
        grid=(n_chunks,),
        in_specs=in_specs,
        out_specs=[blk, pl.BlockSpec((1, N_PAIR, PAIR, PAIR), s_map)],
        out_shape=[jax.ShapeDtypeStruct((n_rows_out, D_A), f32),
                   jax.ShapeDtypeStruct((n_states, N_PAIR, PAIR, PAIR), f32)],
        scratch_shapes=[pltpu.VMEM((N_PAIR, PAIR, PAIR), f32)],
        compiler_params=_cparams("arbitrary"),
        name="rwkv_scan",
    )(*args)


def _pair_states(state):
    b = state.shape[0]
    s = state.reshape(b, N_PAIR, 2, HEAD_DIM, HEAD_DIM)
    out = jnp.zeros((b, N_PAIR, 2, HEAD_DIM, 2, HEAD_DIM), f32)
    out = out.at[:, :, 0, :, 0, :].set(s[:, :, 0]).at[:, :, 1, :, 1, :].set(s[:, :, 1])
    return out.reshape(b, N_PAIR, PAIR, PAIR)


def _unpair_states(s):
    b = s.shape[0]
    s = s.reshape(b, N_PAIR, 2, HEAD_DIM, 2, HEAD_DIM)
    return jnp.stack([s[:, :, 0, :, 0, :], s[:, :, 1, :, 1, :]], axis=2).reshape(b, H_A, HEAD_DIM, HEAD_DIM)


def _rwkv_mixer(pa, state_shift, state_wkv, prm, j, n_sample, n_valid, dec_seq):
    n = pa.shape[0]
    db = n_sample // dec_seq
    ss_rows = jnp.zeros((db, dec_seq, A_IN), f32).at[:, 0].set(state_shift).reshape(n_sample, A_IN)
    vecs = jnp.zeros((8, D_A), f32)
    for idx, name in enumerate(("w0", "a0", "k_k", "k_a", "r_k")):
        vecs = vecs.at[idx].set(prm[name][j].reshape(D_A))
    zpad = jnp.zeros((LORA_W, D_A), f32)
    w2p = jnp.concatenate([prm["w2"][j], zpad], axis=0)
    a2p = jnp.concatenate([zpad, prm["a2"][j]], axis=0)
    arrs = _rwkv_prep(pa, ss_rows, prm["mu_shift"][j].reshape(1, A_IN), vecs, w2p, a2p, prm["g2"][j],
                      n_sample, n_valid, dec_seq)
    gn = jnp.stack([prm["gn_g"][j], prm["gn_b"][j]])
    n_chunks = -(-(n_valid - n_sample) // SCAN_C)
    assert n_sample % SCAN_C == 0 and n_sample + n_chunks * SCAN_C <= n and dec_seq <= SCAN_C
    ya_p, s_p = _rwkv_scan(arrs, gn, n, n_sample // SCAN_C, n_chunks)

    def pad_seq(x):
        x = x[:n_sample].reshape(db, dec_seq, D_A)
        return jnp.pad(x, ((0, 0), (0, SCAN_C - dec_seq), (0, 0))).reshape(db * SCAN_C, D_A)

    ya_s, s_s = _rwkv_scan([pad_seq(x) for x in arrs], gn, db * SCAN_C, 0, db, s0=_pair_states(state_wkv))
    ya_s = ya_s.reshape(db, SCAN_C, D_A)[:, :dec_seq].reshape(n_sample, D_A)
    ya = jnp.concatenate([ya_s, ya_p[n_sample:]], axis=0)
    return ya, _unpair_states(s_p[0][None]), _unpair_states(s_s)


CUM_BLOCK = 256


def _upper_ones(n):
    r = lax.broadcasted_iota(jnp.int32, (n, n), 0)
    c = lax.broadcasted_iota(jnp.int32, (n, n), 1)
    return jnp.where(r <= c, 1.0, 0.0).astype(bf16)


def _cumsum_kernel(x_ref, d_ref, c_ref, carry):
    @pl.when(pl.program_id(0) == 0)
    def _():
        carry[...] = jnp.zeros_like(carry)

    x = x_ref[...]
    n = x.shape[1]
    d_ref[...] = _dot_exact_rhs(x, _upper_ones(n))
    c_ref[0] = carry[...]
    carry[...] = carry[...] + _dot_exact_rhs(x, jnp.ones((n, LANES), bf16))


def _logf_cumsum(lf_t, first_block):
    h, n = lf_t.shape
    nb = n // CUM_BLOCK
    return pl.pallas_call(
        _cumsum_kernel,
        grid=(nb - first_block,),
        in_specs=[pl.BlockSpec((h, CUM_BLOCK), lambda i: (0, i + first_block))],
        out_specs=[pl.BlockSpec((h, CUM_BLOCK), lambda i: (0, i + first_block)),
                   pl.BlockSpec((1, h, LANES), lambda i: (i + first_block, 0, 0))],
        out_shape=[jax.ShapeDtypeStruct((h, n), f32), jax.ShapeDtypeStruct((nb, h, LANES), f32)],
        scratch_shapes=[pltpu.VMEM((h, LANES), f32)],
        compiler_params=_cparams("arbitrary"),
        name="logf_cumsum",
    )(lf_t)


FOX_KW = 2 * PAIR


def _fox_prompt_kernel(row0, cref_ref, q_ref, k_ref, va_ref, vb_ref, o_ref):
    pair = pl.program_id(0)
    i = pl.program_id(1)
    q = q_ref[...]
    lane = lax.broadcasted_iota(jnp.int32, (FOX_TQ, PAIR), 1)
    lane_a = lane < HEAD_DIM
    zero = jnp.zeros_like(q)

    def query_t(qh, col):
        bias = jnp.where(jnp.logical_or(lane == col, lane == col + 1), -1.0, 0.0)
        qa = jnp.concatenate([qh.astype(f32), bias], axis=1)
        return qa.T.astype(bf16)

    q_t = (query_t(jnp.where(lane_a, q, zero), 0), query_t(jnp.where(lane_a, zero, q), 2))
    v_refs = (va_ref, vb_ref)
    key_idx = lax.broadcasted_iota(jnp.int32, (FOX_TK, 1), 0)
    qry_idx = lax.broadcasted_iota(jnp.int32, (1, FOX_TQ), 1)
    blk0 = row0 // FOX_TK

    def block_start(j):
        return pl.multiple_of(row0 + j * FOX_TK, FOX_TK)

    def scores(j):
        kb = k_ref[pl.ds(block_start(j), FOX_TK), :]
        return tuple(_dot(kb, q_t[hh]) for hh in range(2))

    def update(j, s_pair, carry, masked):
        ps, ms, alphas = [], [], []
        for hh in range(2):
            m = carry[hh][0]
            s = s_pair[hh]
            if masked:
                s = jnp.where(key_idx <= qry_idx, s, NEG)
            cref = cref_ref[(blk0 + j) * H_B + 2 * pair + hh]
            m_new = jnp.maximum(m, jnp.max(s, axis=0, keepdims=True) - cref)
            ps.append(jnp.exp((s - (m_new + cref)).astype(bf16)))
            ms.append(m_new)
            alphas.append(jnp.exp(m - m_new))
        start = block_start(j)
        pvs = [_dot(v_refs[hh][pl.ds(start, FOX_TK), :], ps[hh], TN) for hh in range(2)]
        return tuple((ms[hh], carry[hh][1] * alphas[hh] + pvs[hh]) for hh in range(2))

    def step(j, state):
        carry, s_cur = state
        s_next = scores(j + 1)
        return update(j, s_cur, carry, False), s_next

    init = tuple((jnp.full((1, FOX_TQ), NEG, f32), jnp.zeros((PAIR, FOX_TQ), f32)) for _ in range(2))
    carry, s_last = lax.fori_loop(0, i, step, (init, scores(0)))
    (_, acc_a), (_, acc_b) = update(i, s_last, carry, True)
    out_t = jnp.concatenate([acc_a[:HEAD_DIM] / acc_a[HEAD_DIM:HEAD_DIM + 1],
                             acc_b[HEAD_DIM:] / acc_b[0:1]], axis=0)
    o_ref[...] = out_t.T


def _fox_prompt(q16, k_aug, v_a, v_b, cref, n_sample, n_valid):
    n = q16.shape[0]
    assert FOX_TQ == FOX_TK == CUM_BLOCK and n_sample % FOX_TQ == 0
    nq = -(-(n_valid - n_sample) // FOX_TQ)
    assert n_sample + nq * FOX_TQ <= n
    qb0 = n_sample // FOX_TQ
    nb = D_B // PAIR
    return pl.pallas_call(
        functools.partial(_fox_prompt_kernel, n_sample),
        grid=(nb, nq),
        in_specs=[
            pl.BlockSpec(memory_space=pltpu.SMEM),
            pl.BlockSpec((FOX_TQ, PAIR), lambda p, i: (i + qb0, p)),
            pl.BlockSpec((n, FOX_KW), lambda p, i: (0, p)),
            pl.BlockSpec((n, PAIR), lambda p, i: (0, p)),
            pl.BlockSpec((n, PAIR), lambda p, i: (0, p)),
        ],
        out_specs=pl.BlockSpec((FOX_TQ, PAIR), lambda p, i: (i + qb0, p)),
        out_shape=jax.ShapeDtypeStruct((n, D_B), f32),
        compiler_params=_cparams("parallel", "arbitrary"),
        name="fox_prompt",
    )(cref, q16, k_aug, v_a, v_b)


def _fox_sample_kernel(n_steps, page_size, pt_ref, q_ref, kn_ref, vn_ref, lfn_ref, *refs):
    np_ = PAGES_PER_STEP
    k_refs, v_refs, lf_refs = refs[:np_], refs[np_:2 * np_], refs[2 * np_:3 * np_]
    o_ref, m_scr, l_scr, acc_scr, carry_scr = refs[3 * np_:]
    j = pl.program_id(1)
    ds = q_ref.shape[1]
    nrow = H_B * ds

    @pl.when(j == 0)
    def _():
        m_scr[...] = jnp.full(m_scr.shape, NEG, f32)
        l_scr[...] = jnp.zeros(l_scr.shape, f32)
        acc_scr[...] = jnp.zeros(acc_scr.shape, f32)
        carry_scr[...] = jnp.zeros(carry_scr.shape, f32)

    row_h = lax.broadcasted_iota(jnp.int32, (nrow, D_B), 0) // ds
    lane_h = lax.broadcasted_iota(jnp.int32, (nrow, D_B), 1) // HEAD_DIM
    head_sel = row_h == lane_h
    q = q_ref[0]
    qx = jnp.where(head_sel, jnp.broadcast_to(q[None], (H_B, ds, D_B)).reshape(nrow, D_B), 0.0).astype(bf16)
    upper = _upper_ones(page_size)
    all_ones = jnp.ones((page_size, LANES), bf16)

    def attend(ks, vs, lfs, mask):
        nk = page_size * len(ks)
        run = carry_scr[...]
        cums = []
        for lfp in lfs:
            cums.append(_dot_exact_rhs(lfp, upper) + run)
            run = run + _dot_exact_rhs(lfp, all_ones)
        carry_scr[...] = run
        cum = cums[0] if len(cums) == 1 else jnp.concatenate(cums, axis=1)
        cat = lambda xs: (xs[0] if len(xs) == 1 else jnp.concatenate(xs, axis=0)).astype(bf16)
        s = _dot(qx, cat(ks), NT)
        s = s - jnp.broadcast_to(cum[:, None, :], (H_B, ds, nk)).reshape(nrow, nk)
        if mask is not None:
            s = jnp.where(mask, s, NEG)
        m = m_scr[...]
        m_new = jnp.maximum(m, jnp.max(s, axis=1, keepdims=True))
        alpha = jnp.exp(m - m_new)
        p = jnp.exp(s - m_new)
        l_scr[...] = l_scr[...] * alpha + jnp.sum(p, axis=1, keepdims=True)
        acc_scr[...] = acc_scr[...] * alpha + _dot(p.astype(bf16), cat(vs))
        m_scr[...] = m_new

    attend([r[0] for r in k_refs], [r[0] for r in v_refs], [r[0] for r in lf_refs], None)

    @pl.when(j == n_steps - 1)
    def _():
        pad = jnp.zeros((page_size - ds, D_B), f32)
        t_q = lax.broadcasted_iota(jnp.int32, (nrow, page_size), 0) % ds
        t_k = lax.broadcasted_iota(jnp.int32, (nrow, page_size), 1)
        attend([jnp.concatenate([kn_ref[...], pad], axis=0)], [jnp.concatenate([vn_ref[...], pad], axis=0)],
               [lfn_ref[0]], t_k <= t_q)
        o = jnp.where(head_sel, acc_scr[...] / l_scr[...], 0.0)
        out = o[0:ds]
        for h in range(1, H_B):
            out = out + o[h * ds:(h + 1) * ds]
        o_ref[...] = out


def _fox_sample(q_s, kv, lfn_t, cache_k, cache_v, cache_lf_t, page_table, dec_seq):
    db = q_s.shape[0]
    n_pages = page_table.shape[1]
    page_size = cache_k.shape[1]
    assert n_pages % PAGES_PER_STEP == 0 and page_size == LANES
    n_steps = n_pages // PAGES_PER_STEP

    def page_map(pp):
        return lambda b, j, pt: (pt[b, j * PAGES_PER_STEP + pp], 0, 0)

    in_specs = [
        pl.BlockSpec((1, dec_seq, D_B), lambda b, j, pt: (b, 0, 0)),
        pl.BlockSpec((dec_seq, D_B), lambda b, j, pt: (b, 0)),
        pl.BlockSpec((dec_seq, D_B), lambda b, j, pt: (b, 1)),
        pl.BlockSpec((1, H_B, page_size), lambda b, j, pt: (b, 0, 0)),
    ]
    in_specs += [pl.BlockSpec((1, page_size, D_B), page_map(pp)) for pp in range(PAGES_PER_STEP)]
    in_specs += [pl.BlockSpec((1, page_size, D_B), page_map(pp)) for pp in range(PAGES_PER_STEP)]
    in_specs += [pl.BlockSpec((1, H_B, page_size), page_map(pp)) for pp in range(PAGES_PER_STEP)]
    nrow = H_B * dec_seq
    grid_spec = pltpu.PrefetchScalarGridSpec(
        num_scalar_prefetch=1,
        grid=(db, n_steps),
        in_specs=in_specs,
        out_specs=pl.BlockSpec((dec_seq, D_B), lambda b, j, pt: (b, 0)),
        scratch_shapes=[pltpu.VMEM((nrow, 1), f32), pltpu.VMEM((nrow, 1), f32),
                        pltpu.VMEM((nrow, D_B), f32), pltpu.VMEM((H_B, LANES), f32)],
    )
    return pl.pallas_call(
        functools.partial(_fox_sample_kernel, n_steps, page_size),
        grid_spec=grid_spec,
        out_shape=jax.ShapeDtypeStruct((db * dec_seq, D_B), f32),
        compiler_params=_cparams("parallel", "arbitrary"),
        name="fox_sample",
    )(page_table, q_s, kv, kv, lfn_t, *([cache_k] * PAGES_PER_STEP), *([cache_v] * PAGES_PER_STEP),
      *([cache_lf_t] * PAGES_PER_STEP))


def _fox_attention(qkv16, kv, lf, cache_k, cache_v, cache_lf, page_table, n_sample, n_valid, dec_seq):
    n = qkv16.shape[0]
    db = n_sample // dec_seq
    lf_t = lf[:, :H_B].T
    delta_t, c_blk = _logf_cumsum(lf_t, n_sample // CUM_BLOCK)
    delta = delta_t.T
    d_hi = delta.astype(bf16)
    d_lo = (delta - d_hi.astype(f32)).astype(bf16)
    k16, v16 = qkv16[:, D_B:2 * D_B], qkv16[:, 2 * D_B:]
    zpad = jnp.zeros((n, FOX_KW - PAIR - 4), bf16)
    parts = []
    for p in range(D_B // PAIR):
        parts += [k16[:, p * PAIR:(p + 1) * PAIR], d_hi[:, 2 * p:2 * p + 1], d_lo[:, 2 * p:2 * p + 1],
                  d_hi[:, 2 * p + 1:2 * p + 2], d_lo[:, 2 * p + 1:2 * p + 2], zpad]
    k_aug = jnp.concatenate(parts, axis=1)
    lane_a = (jnp.arange(D_B) % PAIR) < HEAD_DIM
    one = jnp.ones((), bf16)
    yb_p = _fox_prompt(qkv16, k_aug, jnp.where(lane_a, v16, one), jnp.where(lane_a, one, v16),
                       c_blk[:, :, 0].reshape(-1), n_sample, n_valid)
    n_pool, page_size = cache_k.shape[0], cache_k.shape[1]
    q_s = qkv16[:n_sample, :D_B].astype(f32).reshape(db, dec_seq, D_B)
    lfn_t = jnp.swapaxes(lf[:n_sample, :H_B].reshape(db, dec_seq, H_B), 1, 2)
    lfn_t = jnp.pad(lfn_t, ((0, 0), (0, 0), (0, page_size - dec_seq)))
    yb_s = _fox_sample(q_s, kv, lfn_t,
                       cache_k.reshape(n_pool, page_size, D_B), cache_v.reshape(n_pool, page_size, D_B),
                       jnp.swapaxes(cache_lf, 1, 2), page_table, dec_seq)
    return jnp.concatenate([yb_s, yb_p[n_sample:]], axis=0)


def _outproj_kernel(alpha, ya_ref, yb_ref, h_ref, w_ref, ln_ref, o_ref):
    y = _dot(ya_ref[...].astype(bf16), w_ref[:D_A, :]) + _dot(yb_ref[...].astype(bf16), w_ref[D_A:, :])
    o_ref[...] = _layer_norm(alpha * h_ref[...] + y, ln_ref[0:1, :], ln_ref[1:2, :])


def _out_projection(ya, yb, h, w_bf, ln, alpha):
    n = h.shape[0]
    half = pl.BlockSpec((ROW_TILE, D_A), lambda i: (i, 0))
    full = pl.BlockSpec((ROW_TILE, D_MODEL), lambda i: (i, 0))
    return pl.pallas_call(
        functools.partial(_outproj_kernel, alpha),
        grid=(n // ROW_TILE,),
        in_specs=[half, half, full,
                  pl.BlockSpec((D_MODEL, D_MODEL), lambda i: (0, 0)),
                  pl.BlockSpec((2, D_MODEL), lambda i: (0, 0))],
        out_specs=full,
        out_shape=jax.ShapeDtypeStruct((n, D_MODEL), f32),
        compiler_params=_cparams("parallel"),
        name="out_projection",
    )(ya, yb, h, w_bf, ln)


POOL_HIST = POOL_BUF + 1


def _pool_kernel(prompt_mode, zero_hist_tile, row0, alpha, x_ref, hist_ref, w_ref, sc_ref, ln_ref, o_ref):
    i = pl.program_id(0) + zero_hist_tile
    tm = x_ref.shape[0]
    x = x_ref[...]
    hist = jnp.where(i == zero_hist_tile, 0.0, hist_ref[...])
    ext = jnp.concatenate([hist, x], axis=0)
    t = i * tm + lax.broadcasted_iota(jnp.int32, (tm, 1), 0) - row0
    ys = []
    for gi, win in enumerate(POOL_WINDOWS):
        s = ext[:, gi * POOL_G:(gi + 1) * POOL_G]
        span = 1
        while span < win:
            s = s + pltpu.roll(s, span, 0)
            span *= 2
        cnt = jnp.minimum(win, t + 1).astype(f32) if prompt_mode else float(win)
        diff = s[POOL_HIST:] / cnt - x[:, gi * POOL_G:(gi + 1) * POOL_G]
        ys.append(_dot(diff.astype(bf16), w_ref[gi]))
    y = jnp.concatenate(ys, axis=1) * sc_ref[...]
    o_ref[...] = _layer_norm(alpha * x + y, ln_ref[0:1, :], ln_ref[1:2, :])


def _pool_mixer(x, w_bf, scale, ln, alpha, tm, first_tile, n_tiles, prompt_mode, row0):
    n = x.shape[0]
    assert tm % POOL_HIST == 0
    hb = tm // POOL_HIST
    return pl.pallas_call(
        functools.partial(_pool_kernel, prompt_mode, first_tile, row0, alpha),
        grid=(n_tiles,),
        in_specs=[
            pl.BlockSpec((tm, D_MODEL), lambda i: (i + first_tile, 0)),
            pl.BlockSpec((POOL_HIST, D_MODEL), lambda i: (jnp.maximum((i + first_tile) * hb - 1, 0), 0)),
            pl.BlockSpec((len(POOL_WINDOWS), POOL_G, POOL_G), lambda i: (0, 0, 0)),
            pl.BlockSpec((1, D_MODEL), lambda i: (0, 0)),
            pl.BlockSpec((2, D_MODEL), lambda i: (0, 0)),
        ],
        out_specs=pl.BlockSpec((tm, D_MODEL), lambda i: (i + first_tile, 0)),
        out_shape=jax.ShapeDtypeStruct((n, D_MODEL), f32),
        compiler_params=_cparams("parallel"),
        name="pool_mixer",
    )(x, x, w_bf, scale, ln)


def _peer_query_kernel(x_ref, wh_ref, wl_ref, o_ref):
    xh, xl = _split(x_ref[...])
    o_ref[...] = _dot(xh, wh_ref[...]) + (_dot(xl, wh_ref[...]) + _dot(xh, wl_ref[...]))


def _peer_query(x, wq_hi, wq_lo):
    n = x.shape[0]
    nq = wq_hi.shape[1]
    return pl.pallas_call(
        _peer_query_kernel,
        grid=(n // ROW_TILE,),
        in_specs=[pl.BlockSpec((ROW_TILE, D_MODEL), lambda i: (i, 0)),
                  pl.BlockSpec((D_MODEL, nq), lambda i: (0, 0)),
                  pl.BlockSpec((D_MODEL, nq), lambda i: (0, 0))],
        out_specs=pl.BlockSpec((ROW_TILE, nq), lambda i: (i, 0)),
        out_shape=jax.ShapeDtypeStruct((n, nq), f32),
        compiler_params=_cparams("parallel"),
        name="peer_query",
    )(x, wq_hi, wq_lo)


def _transpose_cast_kernel(x_ref, o_ref):
    o_ref[...] = x_ref[...].T.astype(bf16)


def _transpose_bf16(x):
    e, d = x.shape
    return pl.pallas_call(
        _transpose_cast_kernel,
        grid=(e // PEER_TE,),
        in_specs=[pl.BlockSpec((PEER_TE, d), lambda i: (i, 0))],
        out_specs=pl.BlockSpec((d, PEER_TE), lambda i: (0, i)),
        out_shape=jax.ShapeDtypeStruct((d, e), bf16),
        compiler_params=_cparams("parallel"),
        name="transpose_bf16",
    )(x)


def _top_values(s, n):
    vals = []
    for _ in range(n):
        m = jnp.max(s, axis=0, keepdims=True)
        vals.append(m)
        s = jnp.where(s == m, -jnp.inf, s)
    return jnp.concatenate(vals, axis=0)


def _peer_router(s1, s2):
    v1 = _top_values(s1, PEER_TOPK)
    v2 = _top_values(s2, PEER_TOPK)
    n_take = PEER_TOPK + 1
    cands = [v1[0:1] + v2]
    sub = lax.broadcasted_iota(jnp.int32, (8, 1), 0)
    for a in range(1, PEER_TOPK):
        lim = n_take // (a + 1)
        assert lim <= 8
        ca = v1[a:a + 1] + v2[0:8]
        cands.append(ca if lim == 8 else jnp.where(sub < lim, ca, -jnp.inf))
    cand = jnp.concatenate(cands, axis=0)
    top = _top_values(cand, n_take)
    tau = 0.5 * (top[PEER_TOPK - 1:PEER_TOPK] + top[PEER_TOPK:PEER_TOPK + 1])
    z = jnp.sum(jnp.where(cand > tau, jnp.exp(cand - top[0:1]), 0.0), axis=0, keepdims=True)
    last = PEER_TOPK - 1
    e1 = jnp.where(s1 >= v1[last:last + 1], jnp.exp(s1 - v1[0:1]) / z, 0.0)
    p = jnp.where(s2 >= v2[last:last + 1], jnp.exp(s2 - v2[0:1]), 0.0)
    return e1, tau - s1, p


INV_SQRT2 = 1.0 / math.sqrt(2.0)


def _peer_kernel(alpha, n_chunks, x_ref, q_ref, keys_ref, u_ref, vt_ref, ln_ref, o_ref,
                 xt_scr, s2_scr, p_scr, e1_scr, c_scr, acc_scr):
    j = pl.program_id(1)
    tn = x_ref.shape[0]

    @pl.when(j == 0)
    def _():
        xt_scr[...] = x_ref[...].T.astype(bf16)
        acc_scr[...] = jnp.zeros(acc_scr.shape, f32)

        def route(h, carry):
            col = pl.multiple_of(h * 2 * PEER_HALF, 2 * PEER_HALF)
            q1 = q_ref[:, pl.ds(col, PEER_HALF)]
            q2 = q_ref[:, pl.ds(col + PEER_HALF, PEER_HALF)]
            s1 = _dot3(keys_ref[2 * h], q1, NT)
            s2 = _dot3(keys_ref[2 * h + 1], q2, NT)
            e1, c, p = _peer_router(s1, s2)
            s2_scr[h] = s2
            p_scr[h] = p
            e1_scr[h] = e1
            c_scr[h] = c
            return carry

        lax.fori_loop(0, PEER_HEADS, route, 0)

    ht = _dot(u_ref[...], xt_scr[...])
    n_rows = PEER_TE // PEER_NKEYS
    parts = []
    for rr in range(n_rows):
        r = j * n_rows + rr
        w = jnp.zeros((PEER_NKEYS, tn), f32)
        for h in range(PEER_HEADS):
            sel = s2_scr[h] >= c_scr[h, pl.ds(r, 1), :]
            w = w + jnp.where(sel, p_scr[h], 0.0) * e1_scr[h, pl.ds(r, 1), :]
        hh = ht[rr * PEER_NKEYS:(rr + 1) * PEER_NKEYS]
        parts.append((w * (0.5 * hh * (1.0 + lax.erf(hh * INV_SQRT2)))).astype(bf16))
    acc_scr[...] += _dot(vt_ref[...], jnp.concatenate(parts, axis=0))

    @pl.when(j == n_chunks - 1)
    def _():
        y = acc_scr[...].T
        o_ref[...] = _layer_norm(alpha * x_ref[...] + y, ln_ref[0:1, :], ln_ref[1:2, :])


def _peer_ffn(x, wq_hi, wq_lo, keys, u_bf, vt_bf, ln, alpha):
    n = x.shape[0]
    n_exp = u_bf.shape[0]
    assert n % PEER_TN == 0 and n_exp == PEER_NKEYS * PEER_NKEYS and n_exp % PEER_TE == 0
    n_chunks = n_exp // PEER_TE
    q = _peer_query(x, wq_hi, wq_lo)
    nq = q.shape[1]
    head_scr = pltpu.VMEM((PEER_HEADS, PEER_NKEYS, PEER_TN), f32)
    return pl.pallas_call(
        functools.partial(_peer_kernel, alpha, n_chunks),
        grid=(n // PEER_TN, n_chunks),
        in_specs=[
            pl.BlockSpec((PEER_TN, D_MODEL), lambda i, j: (i, 0)),
            pl.BlockSpec((PEER_TN, nq), lambda i, j: (i, 0)),
            pl.BlockSpec((2 * PEER_HEADS, PEER_NKEYS, PEER_HALF), lambda i, j: (0, 0, 0)),
            pl.BlockSpec((PEER_TE, D_MODEL), lambda i, j: (j, 0)),
            pl.BlockSpec((D_MODEL, PEER_TE), lambda i, j: (0, j)),
            pl.BlockSpec((2, D_MODEL), lambda i, j: (0, 0)),
        ],
        out_specs=pl.BlockSpec((PEER_TN, D_MODEL), lambda i, j: (i, 0)),
        out_shape=jax.ShapeDtypeStruct((n, D_MODEL), f32),
        scratch_shapes=[pltpu.VMEM((D_MODEL, PEER_TN), bf16), head_scr, head_scr, head_scr, head_scr,
                        pltpu.VMEM((D_MODEL, PEER_TN), f32)],
        compiler_params=_cparams("parallel", "arbitrary"),
        name="peer_ffn",
    )(x, q, keys, u_bf, vt_bf, ln)


POOL_TM = 256


def kernel(x_prompt, x_sample, cache_k, cache_v, cache_logf, page_table, state_wkv, state_shift, state_pool,
           meta, w_in, b_f, w_o, mu_shift, w0, w2, a0, a2, g2, k_k, k_a, r_k, gn_g, gn_b, w_pool, pool_scale,
           ln_g, ln_b, peer_wq, peer_keys, peer_u, peer_v):
    prm = dict(mu_shift=mu_shift, w0=w0, w2=w2, a0=a0, a2=a2, g2=g2, k_k=k_k, k_a=k_a, r_k=r_k,
               gn_g=gn_g, gn_b=gn_b)
    bp, seq, d = x_prompt.shape
    db, dec_seq, _ = x_sample.shape
    assert bp == 1 and d == D_MODEL
    depth = ln_g.shape[0]
    alpha = (2 * depth) ** 0.25
    n_sample = db * dec_seq
    t_p = seq + N_META
    n_valid = n_sample + t_p
    n_rows = n_sample + -(-t_p // FOX_TQ) * FOX_TQ
    n_rows = -(-n_rows // ROW_TILE) * ROW_TILE
    h = jnp.concatenate([x_sample.reshape(n_sample, d), meta.astype(f32), x_prompt[0],
                         jnp.zeros((n_rows - n_valid, d), f32)], axis=0)

    ks, vs, lfs, wkvs, shifts, pools = ([[], []] for _ in range(6))
    n_qkv = A_IN + 3 * D_B
    for i in range(depth):
        j = i // 2
        ln1 = jnp.stack([ln_g[i, 0], ln_b[i, 0]])
        ln2 = jnp.stack([ln_g[i, 1], ln_b[i, 1]])
        if i % 2 == 0:
            w_cat = jnp.concatenate([w_in[j][:, :n_qkv],
                                     jnp.pad(w_in[j][:, n_qkv:], ((0, 0), (0, LANES - H_B)))], axis=1).astype(bf16)
            b_pad = jnp.pad(b_f[j], (0, LANES - H_B)).reshape(1, LANES)
            pa, kv, qkv16, lf = _in_projection(h, w_cat, b_pad)
            ya, wkv_p, wkv_s = _rwkv_mixer(pa, state_shift[j], state_wkv[j], prm, j, n_sample, n_valid, dec_seq)
            yb = _fox_attention(qkv16, kv, lf, cache_k[j], cache_v[j], cache_logf[j], page_table,
                                n_sample, n_valid, dec_seq)
            h1 = _out_projection(ya, yb, h, w_o[j].astype(bf16), ln1, alpha)
            for grp, lo, hi, lead in ((0, n_sample, n_valid, (bp, t_p)), (1, 0, n_sample, (db, dec_seq))):
                ks[grp].append(kv[lo:hi, :D_B].reshape(lead + (H_B, HEAD_DIM)))
                vs[grp].append(kv[lo:hi, D_B:].reshape(lead + (H_B, HEAD_DIM)))
                lfs[grp].append(lf[lo:hi, :H_B].reshape(lead + (H_B,)))
                shifts[grp].append(pa[lo:hi].reshape(lead + (A_IN,))[:, -1])
            wkvs[0].append(wkv_p)
            wkvs[1].append(wkv_s)
        else:
            ext_s = jnp.concatenate([jnp.zeros((db, POOL_HIST - POOL_BUF, d), f32), state_pool[j],
                                     h[:n_sample].reshape(db, dec_seq, d)], axis=1)
            pools[0].append(h[n_valid - POOL_BUF:n_valid][None])
            pools[1].append(ext_s[:, -POOL_BUF:])
            w_bf = w_pool[j].astype(bf16)
            scale = pool_scale[j].reshape(1, d)
            first = n_sample // POOL_TM
            hp = _pool_mixer(h, w_bf, scale, ln1, alpha, POOL_TM, first, n_rows // POOL_TM - first, True, n_sample)
            ext_len = ext_s.shape[1]
            hs = _pool_mixer(ext_s.reshape(db * ext_len, d), w_bf, scale, ln1, alpha, db * ext_len, 0, 1, False, 0)
            hs = hs.reshape(db, ext_len, d)[:, POOL_HIST:].reshape(n_sample, d)
            h1 = jnp.concatenate([hs, hp[n_sample:]], axis=0)
        wq = peer_wq[i]
        wq_hi = wq.astype(bf16)
        wq_lo = (wq - wq_hi.astype(f32)).astype(bf16)
        h = _peer_ffn(h1, wq_hi, wq_lo, peer_keys[i].reshape(2 * PEER_HEADS, PEER_NKEYS, PEER_HALF),
                      peer_u[i].astype(bf16), _transpose_bf16(peer_v[i]), ln2, alpha)

    y_prompt = h[n_sample + N_META:n_valid][None]
    y_sample = h[:n_sample].reshape(db, dec_seq, d)
    st = lambda xs: jnp.stack(xs)
    return (y_prompt, y_sample,
            st(ks[0]), st(vs[0]), st(lfs[0]), st(wkvs[0]), st(shifts[0]), st(pools[0]),
            st(ks[1]), st(vs[1]), st(lfs[1]), st(wkvs[1]), st(shifts[1]), st(pools[1]))
```

```python
import functools
import math

import jax
import jax.numpy as jnp
from jax import lax
from jax.experimental import pallas as pl
from jax.experimental.pallas import tpu as pltpu

f32 = jnp.float32
bf16 = jnp.bfloat16

D_MODEL = 1024
N_META = 16
HEAD_DIM = 64
D_A = D_MODEL // 2
D_B = D_MODEL - D_A
H_A = D_A // HEAD_DIM
H_B = D_B // HEAD_DIM
LORA_W = 64
LORA_A = 64
LORA_G = 128
A_K = D_A
A_V = 2 * D_A
A_W = 3 * D_A
A_IN = A_W + LORA_W + LORA_A + LORA_G
GN_EPS = HEAD_DIM * 1e-5
POOL_WINDOWS = (2, 4, 8, 16)
POOL_G = D_MODEL // len(POOL_WINDOWS)
POOL_BUF = max(POOL_WINDOWS) - 1
PEER_HEADS = 8
PEER_NKEYS = 128
PEER_TOPK = 16
PEER_HALF = 128
LN_EPS = 1e-5
NEG = -1e30

LANES = 128
PAIR = 2 * HEAD_DIM
N_PAIR = D_A // PAIR
ROW_TILE = 512
SCAN_C = 64
INV_BLOCK = 16
FOX_TQ = 256
FOX_TK = 256
PAGES_PER_STEP = 8
PEER_TN = 256
PEER_TE = 1024
VMEM_LIMIT = 56 * 1024 * 1024


def _cparams(*sem):
    return pltpu.CompilerParams(dimension_semantics=sem, vmem_limit_bytes=VMEM_LIMIT)


def _dot(a, b, dims=(((1,), (0,)), ((), ()))):
    return lax.dot_general(a, b, dims, preferred_element_type=f32)


NT = (((1,), (1,)), ((), ()))
TN = (((0,), (0,)), ((), ()))
NN = (((1,), (0,)), ((), ()))


def _split(x):
    hi = x.astype(bf16)
    lo = (x - hi.astype(f32)).astype(bf16)
    return hi, lo


def _dot3(a, b, dims=NN):
    ah, al = _split(a)
    bh, bl = _split(b)
    return _dot(ah, bh, dims) + (_dot(al, bh, dims) + _dot(ah, bl, dims))


def _dot_exact_rhs(a, b_bf, dims=NN):
    a0 = a.astype(bf16)
    r1 = a - a0.astype(f32)
    a1 = r1.astype(bf16)
    a2 = (r1 - a1.astype(f32)).astype(bf16)
    return _dot(a0, b_bf, dims) + (_dot(a1, b_bf, dims) + _dot(a2, b_bf, dims))


def _dot_exact_lhs(a_bf, b, dims=NN):
    b0 = b.astype(bf16)
    r1 = b - b0.astype(f32)
    b1 = r1.astype(bf16)
    b2 = (r1 - b1.astype(f32)).astype(bf16)
    return _dot(a_bf, b0, dims) + (_dot(a_bf, b1, dims) + _dot(a_bf, b2, dims))


def _layer_norm(x, g, b):
    mu = jnp.mean(x, axis=-1, keepdims=True)
    xc = x - mu
    var = jnp.mean(xc * xc, axis=-1, keepdims=True)
    return xc * lax.rsqrt(var + LN_EPS) * g + b


def _head_ones():
    r = lax.broadcasted_iota(jnp.int32, (D_A, D_A), 0) // HEAD_DIM
    c = lax.broadcasted_iota(jnp.int32, (D_A, D_A), 1) // HEAD_DIM
    return jnp.where(r == c, 1.0, 0.0).astype(bf16)


def _proj_kernel(x_ref, w_ref, bf_ref, pa_ref, kv_ref, qkv16_ref, lf_ref):
    x = x_ref[...].astype(bf16)
    y = _dot(x, w_ref[...])
    pa_ref[...] = y[:, :A_IN]
    q = y[:, A_IN:A_IN + D_B]
    kv = y[:, A_IN + D_B:A_IN + 3 * D_B]
    kv_ref[...] = kv
    qkv16_ref[:, :D_B] = (q * (HEAD_DIM ** -0.5)).astype(bf16)
    qkv16_ref[:, D_B:] = kv.astype(bf16)
    lf_ref[...] = jax.nn.log_sigmoid(y[:, A_IN + 3 * D_B:] + bf_ref[...])


def _in_projection(x, w_bf, b_f_pad):
    n = x.shape[0]
    nw = w_bf.shape[1]
    return pl.pallas_call(
        _proj_kernel,
        grid=(n // ROW_TILE,),
        in_specs=[
            pl.BlockSpec((ROW_TILE, D_MODEL), lambda i: (i, 0)),
            pl.BlockSpec((D_MODEL, nw), lambda i: (0, 0)),
            pl.BlockSpec((1, LANES), lambda i: (0, 0)),
        ],
        out_specs=[
            pl.BlockSpec((ROW_TILE, A_IN), lambda i: (i, 0)),
            pl.BlockSpec((ROW_TILE, 2 * D_B), lambda i: (i, 0)),
            pl.BlockSpec((ROW_TILE, 3 * D_B), lambda i: (i, 0)),
            pl.BlockSpec((ROW_TILE, LANES), lambda i: (i, 0)),
        ],
        out_shape=[
            jax.ShapeDtypeStruct((n, A_IN), f32),
            jax.ShapeDtypeStruct((n, 2 * D_B), f32),
            jax.ShapeDtypeStruct((n, 3 * D_B), bf16),
            jax.ShapeDtypeStruct((n, LANES), f32),
        ],
        compiler_params=_cparams("parallel"),
        name="in_projection",
    )(x, w_bf, b_f_pad)


PREP_TM = 256


def _prep_kernel(n_sample, n_valid, dec_seq,
                 pa_ref, prev_ref, ss_ref, mu_ref, vec_ref, w2_ref, a2_ref, g2_ref,
                 r_o, k_o, v_o, kk_o, b_o, lw_o, g_o, bonus_o):
    i = pl.program_id(0)
    tm = pa_ref.shape[0]
    pa = pa_ref[...]
    local = lax.broadcasted_iota(jnp.int32, (tm, 1), 0)
    row = i * tm + local
    prev = pltpu.roll(pa, 1, 0)
    prev = jnp.where(local == 0, prev_ref[7:8, :], prev)
    prev = jnp.where(row == n_sample, 0.0, prev)
    seq_start = jnp.logical_and(row < n_sample, row % dec_seq == 0)
    prev = jnp.where(seq_start, ss_ref[...], prev)
    xs = pa + (prev - pa) * mu_ref[...]

    w0 = vec_ref[0:1, :]
    a0 = vec_ref[1:2, :]
    k_k = vec_ref[2:3, :]
    k_a = vec_ref[3:4, :]
    r_k = vec_ref[4:5, :]
    r = xs[:, :A_K]
    k = xs[:, A_K:A_V]
    v = xs[:, A_V:A_W]
    wa = xs[:, A_W:A_W + LORA_W + LORA_A]
    xg = xs[:, A_W + LORA_W + LORA_A:]
    w = -jax.nn.softplus(-(w0 + _dot3(jnp.tanh(wa), w2_ref[...]))) - 0.5
    lw = -jnp.exp(w)
    a = jax.nn.sigmoid(a0 + _dot3(wa, a2_ref[...]))
    g = _dot(jax.nn.sigmoid(xg).astype(bf16), g2_ref[...].astype(bf16))
    ones = _head_ones()
    kk = k * k_k
    kk = kk / jnp.maximum(jnp.sqrt(_dot_exact_rhs(kk * kk, ones)), 1e-12)
    k = k * (1.0 + (a - 1.0) * k_a)
    bonus = _dot_exact_rhs(r * k * r_k, ones) * v
    valid = row < n_valid
    r_o[...] = r
    k_o[...] = jnp.where(valid, k, 0.0)
    v_o[...] = jnp.where(valid, v, 0.0)
    kk_o[...] = kk
    b_o[...] = jnp.where(valid, kk * a, 0.0)
    lw_o[...] = jnp.where(valid, lw, 0.0)
    g_o[...] = g
    bonus_o[...] = bonus


def _rwkv_prep(pa, ss_rows, mu, vecs, w2p, a2p, g2, n_sample, n_valid, dec_seq):
    n = pa.shape[0]
    assert n_sample == PREP_TM and n % PREP_TM == 0
    row_spec = pl.BlockSpec((PREP_TM, D_A), lambda i: (i, 0))
    const = lambda shape: pl.BlockSpec(shape, lambda i: (0, 0))
    return pl.pallas_call(
        functools.partial(_prep_kernel, n_sample, n_valid, dec_seq),
        grid=(n // PREP_TM,),
        in_specs=[
            pl.BlockSpec((PREP_TM, A_IN), lambda i: (i, 0)),
            pl.BlockSpec((8, A_IN), lambda i: (jnp.maximum(i * (PREP_TM // 8) - 1, 0), 0)),
            const((PREP_TM, A_IN)),
            const((1, A_IN)),
            const((8, D_A)),
            const((LORA_W + LORA_A, D_A)),
            const((LORA_W + LORA_A, D_A)),
            const((LORA_G, D_A)),
        ],
        out_specs=[row_spec] * 8,
        out_shape=[jax.ShapeDtypeStruct((n, D_A), f32)] * 8,
        compiler_params=_cparams("parallel"),
        name="rwkv_prep",
    )(pa, pa, ss_rows, mu, vecs, w2p, a2p, g2)


def _stack_heads(x, lane_a):
    return jnp.concatenate([jnp.where(lane_a, x, 0.0), jnp.where(lane_a, 0.0, x)], axis=0)


def _unit_lower_inverse(x):
    n = x.shape[0]
    ri = lax.broadcasted_iota(jnp.int32, (n, n), 0)
    ci = lax.broadcasted_iota(jnp.int32, (n, n), 1)
    eye = jnp.where(ri == ci, 1.0, 0.0)
    d = jnp.where(ri // INV_BLOCK == ci // INV_BLOCK, x, 0.0)
    off = x - d
    d2 = _dot3(d, d)
    yield
    d4 = _dot3(d2, d2)
    yield
    d8 = _dot3(d4, d4)
    p = eye - d
    p = p + _dot3(p, d2)
    yield
    p = p + _dot3(p, d4)
    yield
    dinv = p + _dot3(p, d8)
    yield
    m = _dot3(dinv, off)
    yield
    m2 = _dot3(m, m)
    yield
    q = eye - m
    q = q + _dot3(q, m2)
    yield
    return _dot3(q, dinv)


def _scan_pair(r, k, v, kk, b, lw, g, bonus, gn, S):
    C = r.shape[0]
    assert C == SCAN_C and SCAN_C // INV_BLOCK == 4 and 2 * C == PAIR
    lane = lax.broadcasted_iota(jnp.int32, (1, PAIR), 1)
    lane_a = lane < HEAD_DIM
    t_idx = lax.broadcasted_iota(jnp.int32, (C, 1), 0)
    tri = jnp.where(lax.broadcasted_iota(jnp.int32, (C, C), 1) <= t_idx, 1.0, 0.0).astype(bf16)
    cs = _dot_exact_lhs(tri, lw)
    total = cs[C - 1:C, :]
    kp = kk * jnp.exp(cs - lw)
    rp = r * jnp.exp(cs)
    g_inv = jnp.exp(-cs)
    g_tail = jnp.exp(total - cs)
    kt, bt = k * g_inv, b * g_inv
    kh, bh = k * g_tail, b * g_tail

    st = lambda x: _stack_heads(x, lane_a)
    sc = _dot3(jnp.concatenate([kp, rp], axis=0),
               jnp.concatenate([st(bt), st(kt)], axis=0), NT)
    yield
    s_idx = lane % C
    strict = s_idx < t_idx
    incl = s_idx <= t_idx
    lb = jnp.where(strict, sc[:C, :PAIR], 0.0)
    lk = jnp.where(strict, sc[:C, PAIR:], 0.0)
    pb = jnp.where(incl, sc[C:, :PAIR], 0.0)
    pk = jnp.where(incl, sc[C:, PAIR:], 0.0)

    lkv = _dot3(lk, st(v))
    yield
    tinv = yield from _unit_lower_inverse(st(lb))
    yield
    tcat = tinv[:C] + tinv[C:]
    w12 = _dot3(tcat, jnp.concatenate([st(lkv), st(kp)], axis=1))
    yield
    w1, w2 = w12[:, :PAIR], w12[:, PAIR:]
    pbw = _dot(pb.astype(bf16), jnp.concatenate([st(w1), st(w2)], axis=1).astype(bf16))
    pkv = _dot(pk.astype(bf16), st(v).astype(bf16))
    yield
    q2 = rp - pbw[:, PAIR:]
    y = _dot(q2.astype(bf16), S.astype(bf16), NT) + (pkv - pbw[:, :PAIR])

    ri = lax.broadcasted_iota(jnp.int32, (PAIR, PAIR), 0)
    ci = lax.broadcasted_iota(jnp.int32, (PAIR, PAIR), 1)
    same_head = (ri // HEAD_DIM) == (ci // HEAD_DIM)
    wb = _dot3(jnp.concatenate([w2, w1], axis=1), bh, TN)
    yield
    g2m = jnp.where(ri == ci, jnp.exp(total), 0.0) - jnp.where(same_head, wb[:PAIR], 0.0)
    g1m = jnp.where(same_head, _dot3(v, kh, TN) - wb[PAIR:], 0.0)
    yield
    s_new = _dot3(S, g2m) + g1m
    yield

    ones = jnp.where(same_head, 1.0, 0.0).astype(bf16)
    mean = _dot_exact_rhs(y, ones) * (1.0 / HEAD_DIM)
    yc = y - mean
    var = _dot_exact_rhs(yc * yc, ones) * (1.0 / HEAD_DIM)
    yn = yc * lax.rsqrt(var + GN_EPS) * gn[0:1, :] + gn[1:2, :]
    return (yn + bonus) * g, s_new


def _scan_kernel(per_chunk_state, *refs):
    if per_chunk_state:
        (r_ref, k_ref, v_ref, kk_ref, b_ref, lw_ref, g_ref, bonus_ref, gn_ref, s0_ref,
         ya_ref, sout_ref, s_scr) = refs
        s_scr[...] = s0_ref[0]
    else:
        (r_ref, k_ref, v_ref, kk_ref, b_ref, lw_ref, g_ref, bonus_ref, gn_ref,
         ya_ref, sout_ref, s_scr) = refs

        @pl.when(pl.program_id(0) == 0)
        def _():
            s_scr[...] = jnp.zeros(s_scr.shape, f32)

    in_refs = (r_ref, k_ref, v_ref, kk_ref, b_ref, lw_ref, g_ref, bonus_ref, gn_ref)
    lanes = [slice(p * PAIR, (p + 1) * PAIR) for p in range(N_PAIR)]
    chains = [_scan_pair(*(x[:, sl] for x in in_refs), s_scr[p]) for p, sl in enumerate(lanes)]
    pending = list(range(N_PAIR))
    while pending:
        for p in list(pending):
            try:
                next(chains[p])
            except StopIteration as done:
                ya, s_new = done.value
                ya_ref[:, lanes[p]] = ya
                s_scr[p] = s_new
                sout_ref[0, p] = s_new
                pending.remove(p)


def _rwkv_scan(arrs, gn, n_rows_out, row_block_off, n_chunks, s0=None):
    per_chunk = s0 is not None
    blk = pl.BlockSpec((SCAN_C, D_A), lambda c: (c + row_block_off, 0))
    in_specs = [blk] * 8 + [pl.BlockSpec((2, D_A), lambda c: (0, 0))]
    args = list(arrs) + [gn]
    if per_chunk:
        in_specs.append(pl.BlockSpec((1, N_PAIR, PAIR, PAIR), lambda c: (c, 0, 0, 0)))
        args.append(s0)
        n_states = n_chunks
        s_map = lambda c: (c, 0, 0, 0)
    else:
        n_states = 1
        s_map = lambda c: (0, 0, 0, 0)
    return pl.pallas_call(
        functools.partial(_scan_kernel, per_chunk),
        grid=(n_chunks,),
        in_specs=in_specs,
        out_specs=[blk, pl.BlockSpec((1, N_PAIR, PAIR, PAIR), s_map)],
        out_shape=[jax.ShapeDtypeStruct((n_rows_out, D_A), f32),
                   jax.ShapeDtypeStruct((n_states, N_PAIR, PAIR, PAIR), f32)],
        scratch_shapes=[pltpu.VMEM((N_PAIR, PAIR, PAIR), f32)],
        compiler_params=_cparams("arbitrary"),
        name="rwkv_scan",
    )(*args)


def _pair_states(state):
    b = state.shape[0]
    s = state.reshape(b, N_PAIR, 2, HEAD_DIM, HEAD_DIM)
    out = jnp.zeros((b, N_PAIR, 2, HEAD_DIM, 2, HEAD_DIM), f32)
    out = out.at[:, :, 0, :, 0, :].set(s[:, :, 0]).at[:, :, 1, :, 1, :].set(s[:, :, 1])
    return out.reshape(b, N_PAIR, PAIR, PAIR)


def _unpair_states(s):
    b = s.shape[0]
    s = s.reshape(b, N_PAIR, 2, HEAD_DIM, 2, HEAD_DIM)
    return jnp.stack([s[:, :, 0, :, 0, :], s[:, :, 1, :, 1, :]], axis=2).reshape(b, H_A, HEAD_DIM, HEAD_DIM)


def _rwkv_mixer(pa, state_shift, state_wkv, prm, j, n_sample, n_valid, dec_seq):
    n = pa.shape[0]
    db = n_sample // dec_seq
    ss_rows = jnp.zeros((db, dec_seq, A_IN), f32).at[:, 0].set(state_shift).reshape(n_sample, A_IN)
    vecs = jnp.zeros((8, D_A), f32)
    for idx, name in enumerate(("w0", "a0", "k_k", "k_a", "r_k")):
        vecs = vecs.at[idx].set(prm[name][j].reshape(D_A))
    zpad = jnp.zeros((LORA_W, D_A), f32)
    w2p = jnp.concatenate([prm["w2"][j], zpad], axis=0)
    a2p = jnp.concatenate([zpad, prm["a2"][j]], axis=0)
    arrs = _rwkv_prep(pa, ss_rows, prm["mu_shift"][j].reshape(1, A_IN), vecs, w2p, a2p, prm["g2"][j],
                      n_sample, n_valid, dec_seq)
    gn = jnp.stack([prm["gn_g"][j], prm["gn_b"][j]])
    n_chunks = -(-(n_valid - n_sample) // SCAN_C)
    assert n_sample % SCAN_C == 0 and n_sample + n_chunks * SCAN_C <= n and dec_seq <= SCAN_C
    ya_p, s_p = _rwkv_scan(arrs, gn, n, n_sample // SCAN_C, n_chunks)

    def pad_seq(x):
        x = x[:n_sample].reshape(db, dec_seq, D_A)
        return jnp.pad(x, ((0, 0), (0, SCAN_C - dec_seq), (0, 0))).reshape(db * SCAN_C, D_A)

    ya_s, s_s = _rwkv_scan([pad_seq(x) for x in arrs], gn, db * SCAN_C, 0, db, s0=_pair_states(state_wkv))
    ya_s = ya_s.reshape(db, SCAN_C, D_A)[:, :dec_seq].reshape(n_sample, D_A)
    ya = jnp.concatenate([ya_s, ya_p[n_sample:]], axis=0)
    return ya, _unpair_states(s_p[0][None]), _unpair_states(s_s)


CUM_BLOCK = 256


def _upper_ones(n):
    r = lax.broadcasted_iota(jnp.int32, (n, n), 0)
    c = lax.broadcasted_iota(jnp.int32, (n, n), 1)
    return jnp.where(r <= c, 1.0, 0.0).astype(bf16)


def _cumsum_kernel(x_ref, d_ref, c_ref, carry):
    @pl.when(pl.program_id(0) == 0)
    def _():
        carry[...] = jnp.zeros_like(carry)

    x = x_ref[...]
    n = x.shape[1]
    d_ref[...] = _dot_exact_rhs(x, _upper_ones(n))
    c_ref[0] = carry[...]
    carry[...] = carry[...] + _dot_exact_rhs(x, jnp.ones((n, LANES), bf16))


def _logf_cumsum(lf_t, first_block):
    h, n = lf_t.shape
    nb = n // CUM_BLOCK
    return pl.pallas_call(
        _cumsum_kernel,
        grid=(nb - first_block,),
        in_specs=[pl.BlockSpec((h, CUM_BLOCK), lambda i: (0, i + first_block))],
        out_specs=[pl.BlockSpec((h, CUM_BLOCK), lambda i: (0, i + first_block)),
                   pl.BlockSpec((1, h, LANES), lambda i: (i + first_block, 0, 0))],
        out_shape=[jax.ShapeDtypeStruct((h, n), f32), jax.ShapeDtypeStruct((nb, h, LANES), f32)],
        scratch_shapes=[pltpu.VMEM((h, LANES), f32)],
        compiler_params=_cparams("arbitrary"),
        name="logf_cumsum",
    )(lf_t)


FOX_KW = 2 * PAIR


def _fox_prompt_kernel(row0, cref_ref, q_ref, k_ref, va_ref, vb_ref, o_ref):
    pair = pl.program_id(0)
    i = pl.program_id(1)
    q = q_ref[...]
    lane = lax.broadcasted_iota(jnp.int32, (FOX_TQ, PAIR), 1)
    lane_a = lane < HEAD_DIM
    zero = jnp.zeros_like(q)

    def query_t(qh, col):
        bias = jnp.where(jnp.logical_or(lane == col, lane == col + 1), -1.0, 0.0)
        qa = jnp.concatenate([qh.astype(f32), bias], axis=1)
        return qa.T.astype(bf16)

    q_t = (query_t(jnp.where(lane_a, q, zero), 0), query_t(jnp.where(lane_a, zero, q), 2))
    v_refs = (va_ref, vb_ref)
    key_idx = lax.broadcasted_iota(jnp.int32, (FOX_TK, 1), 0)
    qry_idx = lax.broadcasted_iota(jnp.int32, (1, FOX_TQ), 1)
    blk0 = row0 // FOX_TK

    def block_start(j):
        return pl.multiple_of(row0 + j * FOX_TK, FOX_TK)

    def scores(j):
        kb = k_ref[pl.ds(block_start(j), FOX_TK), :]
        return tuple(_dot(kb, q_t[hh]) for hh in range(2))

    def update(j, s_pair, carry, masked):
        ps, ms, alphas = [], [], []
        for hh in range(2):
            m = carry[hh][0]
            s = s_pair[hh]
            if masked:
                s = jnp.where(key_idx <= qry_idx, s, NEG)
            cref = cref_ref[(blk0 + j) * H_B + 2 * pair + hh]
            m_new = jnp.maximum(m, jnp.max(s, axis=0, keepdims=True) - cref)
            ps.append(jnp.exp((s - (m_new + cref)).astype(bf16)))
            ms.append(m_new)
            alphas.append(jnp.exp(m - m_new))
        start = block_start(j)
        pvs = [_dot(v_refs[hh][pl.ds(start, FOX_TK), :], ps[hh], TN) for hh in range(2)]
        return tuple((ms[hh], carry[hh][1] * alphas[hh] + pvs[hh]) for hh in range(2))

    def step(j, state):
        carry, s_cur = state
        s_next = scores(j + 1)
        return update(j, s_cur, carry, False), s_next

    init = tuple((jnp.full((1, FOX_TQ), NEG, f32), jnp.zeros((PAIR, FOX_TQ), f32)) for _ in range(2))
    carry, s_last = lax.fori_loop(0, i, step, (init, scores(0)))
    (_, acc_a), (_, acc_b) = update(i, s_last, carry, True)
    out_t = jnp.concatenate([acc_a[:HEAD_DIM] / acc_a[HEAD_DIM:HEAD_DIM + 1],
                             acc_b[HEAD_DIM:] / acc_b[0:1]], axis=0)
    o_ref[...] = out_t.T


def _fox_prompt(q16, k_aug, v_a, v_b, cref, n_sample, n_valid):
    n = q16.shape[0]
    assert FOX_TQ == FOX_TK == CUM_BLOCK and n_sample % FOX_TQ == 0
    nq = -(-(n_valid - n_sample) // FOX_TQ)
    assert n_sample + nq * FOX_TQ <= n
    qb0 = n_sample // FOX_TQ
    nb = D_B // PAIR
    return pl.pallas_call(
        functools.partial(_fox_prompt_kernel, n_sample),
        grid=(nb, nq),
        in_specs=[
            pl.BlockSpec(memory_space=pltpu.SMEM),
            pl.BlockSpec((FOX_TQ, PAIR), lambda p, i: (i + qb0, p)),
            pl.BlockSpec((n, FOX_KW), lambda p, i: (0, p)),
            pl.BlockSpec((n, PAIR), lambda p, i: (0, p)),
            pl.BlockSpec((n, PAIR), lambda p, i: (0, p)),
        ],
        out_specs=pl.BlockSpec((FOX_TQ, PAIR), lambda p, i: (i + qb0, p)),
        out_shape=jax.ShapeDtypeStruct((n, D_B), f32),
        compiler_params=_cparams("parallel", "arbitrary"),
        name="fox_prompt",
    )(cref, q16, k_aug, v_a, v_b)


def _fox_sample_kernel(n_steps, page_size, pt_ref, q_ref, kn_ref, vn_ref, lfn_ref, *refs):
    np_ = PAGES_PER_STEP
    k_refs, v_refs, lf_refs = refs[:np_], refs[np_:2 * np_], refs[2 * np_:3 * np_]
    o_ref, m_scr, l_scr, acc_scr, carry_scr = refs[3 * np_:]
    j = pl.program_id(1)
    ds = q_ref.shape[1]
    nrow = H_B * ds

    @pl.when(j == 0)
    def _():
        m_scr[...] = jnp.full(m_scr.shape, NEG, f32)
        l_scr[...] = jnp.zeros(l_scr.shape, f32)
        acc_scr[...] = jnp.zeros(acc_scr.shape, f32)
        carry_scr[...] = jnp.zeros(carry_scr.shape, f32)

    row_h = lax.broadcasted_iota(jnp.int32, (nrow, D_B), 0) // ds
    lane_h = lax.broadcasted_iota(jnp.int32, (nrow, D_B), 1) // HEAD_DIM
    head_sel = row_h == lane_h
    q = q_ref[0]
    qx = jnp.where(head_sel, jnp.broadcast_to(q[None], (H_B, ds, D_B)).reshape(nrow, D_B), 0.0).astype(bf16)
    upper = _upper_ones(page_size)
    all_ones = jnp.ones((page_size, LANES), bf16)

    def attend(ks, vs, lfs, mask):
        nk = page_size * len(ks)
        run = carry_scr[...]
        cums = []
        for lfp in lfs:
            cums.append(_dot_exact_rhs(lfp, upper) + run)
            run = run + _dot_exact_rhs(lfp, all_ones)
        carry_scr[...] = run
        cum = cums[0] if len(cums) == 1 else jnp.concatenate(cums, axis=1)
        cat = lambda xs: (xs[0] if len(xs) == 1 else jnp.concatenate(xs, axis=0)).astype(bf16)
        s = _dot(qx, cat(ks), NT)
        s = s - jnp.broadcast_to(cum[:, None, :], (H_B, ds, nk)).reshape(nrow, nk)
        if mask is not None:
            s = jnp.where(mask, s, NEG)
        m = m_scr[...]
        m_new = jnp.maximum(m, jnp.max(s, axis=1, keepdims=True))
        alpha = jnp.exp(m - m_new)
        p = jnp.exp(s - m_new)
        l_scr[...] = l_scr[...] * alpha + jnp.sum(p, axis=1, keepdims=True)
        acc_scr[...] = acc_scr[...] * alpha + _dot(p.astype(bf16), cat(vs))
        m_scr[...] = m_new

    attend([r[0] for r in k_refs], [r[0] for r in v_refs], [r[0] for r in lf_refs], None)

    @pl.when(j == n_steps - 1)
    def _():
        pad = jnp.zeros((page_size - ds, D_B), f32)
        t_q = lax.broadcasted_iota(jnp.int32, (nrow, page_size), 0) % ds
        t_k = lax.broadcasted_iota(jnp.int32, (nrow, page_size), 1)
        attend([jnp.concatenate([kn_ref[...], pad], axis=0)], [jnp.concatenate([vn_ref[...], pad], axis=0)],
               [lfn_ref[0]], t_k <= t_q)
        o = jnp.where(head_sel, acc_scr[...] / l_scr[...], 0.0)
        out = o[0:ds]
        for h in range(1, H_B):
            out = out + o[h * ds:(h + 1) * ds]
        o_ref[...] = out


def _fox_sample(q_s, kv, lfn_t, cache_k, cache_v, cache_lf_t, page_table, dec_seq):
    db = q_s.shape[0]
    n_pages = page_table.shape[1]
    page_size = cache_k.shape[1]
    assert n_pages % PAGES_PER_STEP == 0 and page_size == LANES
    n_steps = n_pages // PAGES_PER_STEP

    def page_map(pp):
        return lambda b, j, pt: (pt[b, j * PAGES_PER_STEP + pp], 0, 0)

    in_specs = [
        pl.BlockSpec((1, dec_seq, D_B), lambda b, j, pt: (b, 0, 0)),
        pl.BlockSpec((dec_seq, D_B), lambda b, j, pt: (b, 0)),
        pl.BlockSpec((dec_seq, D_B), lambda b, j, pt: (b, 1)),
        pl.BlockSpec((1, H_B, page_size), lambda b, j, pt: (b, 0, 0)),
    ]
    in_specs += [pl.BlockSpec((1, page_size, D_B), page_map(pp)) for pp in range(PAGES_PER_STEP)]
    in_specs += [pl.BlockSpec((1, page_size, D_B), page_map(pp)) for pp in range(PAGES_PER_STEP)]
    in_specs += [pl.BlockSpec((1, H_B, page_size), page_map(pp)) for pp in range(PAGES_PER_STEP)]
    nrow = H_B * dec_seq
    grid_spec = pltpu.PrefetchScalarGridSpec(
        num_scalar_prefetch=1,
        grid=(db, n_steps),
        in_specs=in_specs,
        out_specs=pl.BlockSpec((dec_seq, D_B), lambda b, j, pt: (b, 0)),
        scratch_shapes=[pltpu.VMEM((nrow, 1), f32), pltpu.VMEM((nrow, 1), f32),
                        pltpu.VMEM((nrow, D_B), f32), pltpu.VMEM((H_B, LANES), f32)],
    )
    return pl.pallas_call(
        functools.partial(_fox_sample_kernel, n_steps, page_size),
        grid_spec=grid_spec,
        out_shape=jax.ShapeDtypeStruct((db * dec_seq, D_B), f32),
        compiler_params=_cparams("parallel", "arbitrary"),
        name="fox_sample",
    )(page_table, q_s, kv, kv, lfn_t, *([cache_k] * PAGES_PER_STEP), *([cache_v] * PAGES_PER_STEP),
      *([cache_lf_t] * PAGES_PER_STEP))


def _fox_attention(qkv16, kv, lf, cache_k, cache_v, cache_lf, page_table, n_sample, n_valid, dec_seq):
    n = qkv16.shape[0]
    db = n_sample // dec_seq
    lf_t = lf[:, :H_B].T
    delta_t, c_blk = _logf_cumsum(lf_t, n_sample // CUM_BLOCK)
    delta = delta_t.T
    d_hi = delta.astype(bf16)
    d_lo = (delta - d_hi.astype(f32)).astype(bf16)
    k16, v16 = qkv16[:, D_B:2 * D_B], qkv16[:, 2 * D_B:]
    zpad = jnp.zeros((n, FOX_KW - PAIR - 4), bf16)
    parts = []
    for p in range(D_B // PAIR):
        parts += [k16[:, p * PAIR:(p + 1) * PAIR], d_hi[:, 2 * p:2 * p + 1], d_lo[:, 2 * p:2 * p + 1],
                  d_hi[:, 2 * p + 1:2 * p + 2], d_lo[:, 2 * p + 1:2 * p + 2], zpad]
    k_aug = jnp.concatenate(parts, axis=1)
    lane_a = (jnp.arange(D_B) % PAIR) < HEAD_DIM
    one = jnp.ones((), bf16)
    yb_p = _fox_prompt(qkv16, k_aug, jnp.where(lane_a, v16, one), jnp.where(lane_a, one, v16),
                       c_blk[:, :, 0].reshape(-1), n_sample, n_valid)
    n_pool, page_size = cache_k.shape[0], cache_k.shape[1]
    q_s = qkv16[:n_sample, :D_B].astype(f32).reshape(db, dec_seq, D_B)
    lfn_t = jnp.swapaxes(lf[:n_sample, :H_B].reshape(db, dec_seq, H_B), 1, 2)
    lfn_t = jnp.pad(lfn_t, ((0, 0), (0, 0), (0, page_size - dec_seq)))
    yb_s = _fox_sample(q_s, kv, lfn_t,
                       cache_k.reshape(n_pool, page_size, D_B), cache_v.reshape(n_pool, page_size, D_B),
                       jnp.swapaxes(cache_lf, 1, 2), page_table, dec_seq)
    return jnp.concatenate([yb_s, yb_p[n_sample:]], axis=0)


def _outproj_kernel(alpha, ya_ref, yb_ref, h_ref, w_ref, ln_ref, o_ref):
    y = _dot(ya_ref[...].astype(bf16), w_ref[:D_A, :]) + _dot(yb_ref[...].astype(bf16), w_ref[D_A:, :])
    o_ref[...] = _layer_norm(alpha * h_ref[...] + y, ln_ref[0:1, :], ln_ref[1:2, :])


def _out_projection(ya, yb, h, w_bf, ln, alpha):
    n = h.shape[0]
    half = pl.BlockSpec((ROW_TILE, D_A), lambda i: (i, 0))
    full = pl.BlockSpec((ROW_TILE, D_MODEL), lambda i: (i, 0))
    return pl.pallas_call(
        functools.partial(_outproj_kernel, alpha),
        grid=(n // ROW_TILE,),
        in_specs=[half, half, full,
                  pl.BlockSpec((D_MODEL, D_MODEL), lambda i: (0, 0)),
                  pl.BlockSpec((2, D_MODEL), lambda i: (0, 0))],
        out_specs=full,
        out_shape=jax.ShapeDtypeStruct((n, D_MODEL), f32),
        compiler_params=_cparams("parallel"),
        name="out_projection",
    )(ya, yb, h, w_bf, ln)


POOL_HIST = POOL_BUF + 1


def _pool_kernel(prompt_mode, zero_hist_tile, row0, alpha, x_ref, hist_ref, w_ref, sc_ref, ln_ref, o_ref):
    i = pl.program_id(0) + zero_hist_tile
    tm = x_ref.shape[0]
    x = x_ref[...]
    hist = jnp.where(i == zero_hist_tile, 0.0, hist_ref[...])
    ext = jnp.concatenate([hist, x], axis=0)
    t = i * tm + lax.broadcasted_iota(jnp.int32, (tm, 1), 0) - row0
    ys = []
    for gi, win in enumerate(POOL_WINDOWS):
        s = ext[:, gi * POOL_G:(gi + 1) * POOL_G]
        span = 1
        while span < win:
            s = s + pltpu.roll(s, span, 0)
            span *= 2
        cnt = jnp.minimum(win, t + 1).astype(f32) if prompt_mode else float(win)
        diff = s[POOL_HIST:] / cnt - x[:, gi * POOL_G:(gi + 1) * POOL_G]
        ys.append(_dot(diff.astype(bf16), w_ref[gi]))
    y = jnp.concatenate(ys, axis=1) * sc_ref[...]
    o_ref[...] = _layer_norm(alpha * x + y, ln_ref[0:1, :], ln_ref[1:2, :])


def _pool_mixer(x, w_bf, scale, ln, alpha, tm, first_tile, n_tiles, prompt_mode, row0):
    n = x.shape[0]
    assert tm % POOL_HIST == 0
    hb = tm // POOL_HIST
    return pl.pallas_call(
        functools.partial(_pool_kernel, prompt_mode, first_tile, row0, alpha),
        grid=(n_tiles,),
        in_specs=[
            pl.BlockSpec((tm, D_MODEL), lambda i: (i + first_tile, 0)),
            pl.BlockSpec((POOL_HIST, D_MODEL), lambda i: (jnp.maximum((i + first_tile) * hb - 1, 0), 0)),
            pl.BlockSpec((len(POOL_WINDOWS), POOL_G, POOL_G), lambda i: (0, 0, 0)),
            pl.BlockSpec((1, D_MODEL), lambda i: (0, 0)),
            pl.BlockSpec((2, D_MODEL), lambda i: (0, 0)),
        ],
        out_specs=pl.BlockSpec((tm, D_MODEL), lambda i: (i + first_tile, 0)),
        out_shape=jax.ShapeDtypeStruct((n, D_MODEL), f32),
        compiler_params=_cparams("parallel"),
        name="pool_mixer",
    )(x, x, w_bf, scale, ln)


def _peer_query_kernel(x_ref, wh_ref, wl_ref, o_ref):
    xh, xl = _split(x_ref[...])
    o_ref[...] = _dot(xh, wh_ref[...]) + (_dot(xl, wh_ref[...]) + _dot(xh, wl_ref[...]))


def _peer_query(x, wq_hi, wq_lo):
    n = x.shape[0]
    nq = wq_hi.shape[1]
    return pl.pallas_call(
        _peer_query_kernel,
        grid=(n // ROW_TILE,),
        in_specs=[pl.BlockSpec((ROW_TILE, D_MODEL), lambda i: (i, 0)),
                  pl.BlockSpec((D_MODEL, nq), lambda i: (0, 0)),
                  pl.BlockSpec((D_MODEL, nq), lambda i: (0, 0))],
        out_specs=pl.BlockSpec((ROW_TILE, nq), lambda i: (i, 0)),
        out_shape=jax.ShapeDtypeStruct((n, nq), f32),
        compiler_params=_cparams("parallel"),
        name="peer_query",
    )(x, wq_hi, wq_lo)


def _transpose_cast_kernel(x_ref, o_ref):
    o_ref[...] = x_ref[...].T.astype(bf16)


def _transpose_bf16(x):
    e, d = x.shape
    return pl.pallas_call(
        _transpose_cast_kernel,
        grid=(e // PEER_TE,),
        in_specs=[pl.BlockSpec((PEER_TE, d), lambda i: (i, 0))],
        out_specs=pl.BlockSpec((d, PEER_TE), lambda i: (0, i)),
        out_shape=jax.ShapeDtypeStruct((d, e), bf16),
        compiler_params=_cparams("parallel"),
        name="transpose_bf16",
    )(x)


def _top_values(s, n):
    vals = []
    for _ in range(n):
        m = jnp.max(s, axis=0, keepdims=True)
        vals.append(m)
        s = jnp.where(s == m, -jnp.inf, s)
    return jnp.concatenate(vals, axis=0)


def _peer_router(s1, s2):
    v1 = _top_values(s1, PEER_TOPK)
    v2 = _top_values(s2, PEER_TOPK)
    n_take = PEER_TOPK + 1
    cands = [v1[0:1] + v2]
    sub = lax.broadcasted_iota(jnp.int32, (8, 1), 0)
    for a in range(1, PEER_TOPK):
        lim = n_take // (a + 1)
        assert lim <= 8
        ca = v1[a:a + 1] + v2[0:8]
        cands.append(ca if lim == 8 else jnp.where(sub < lim, ca, -jnp.inf))
    cand = jnp.concatenate(cands, axis=0)
    top = _top_values(cand, n_take)
    tau = 0.5 * (top[PEER_TOPK - 1:PEER_TOPK] + top[PEER_TOPK:PEER_TOPK + 1])
    z = jnp.sum(jnp.where(cand > tau, jnp.exp(cand - top[0:1]), 0.0), axis=0, keepdims=True)
    last = PEER_TOPK - 1
    e1 = jnp.where(s1 >= v1[last:last + 1], jnp.exp(s1 - v1[0:1]) / z, 0.0)
    p = jnp.where(s2 >= v2[last:last + 1], jnp.exp(s2 - v2[0:1]), 0.0)
    return e1, tau - s1, p


INV_SQRT2 = 1.0 / math.sqrt(2.0)


def _peer_kernel(alpha, n_chunks, x_ref, q_ref, keys_ref, u_ref, vt_ref, ln_ref, o_ref,
                 xt_scr, s2_scr, p_scr, e1_scr, c_scr, acc_scr, g_scr, ht_scr, w_scr):
    j = pl.program_id(1)
    tn = x_ref.shape[0]

    @pl.when(j == 0)
    def _():
        xt_scr[...] = x_ref[...].T.astype(bf16)
        acc_scr[...] = jnp.zeros(acc_scr.shape, f32)
        g_scr[...] = jnp.zeros(g_scr.shape, bf16)
        ht_scr[...] = jnp.zeros(ht_scr.shape, f32)
        w_scr[...] = jnp.zeros(w_scr.shape, f32)

        def route(h, carry):
            col = pl.multiple_of(h * 2 * PEER_HALF, 2 * PEER_HALF)
            q1 = q_ref[:, pl.ds(col, PEER_HALF)]
            q2 = q_ref[:, pl.ds(col + PEER_HALF, PEER_HALF)]
            s1 = _dot3(keys_ref[2 * h], q1, NT)
            s2 = _dot3(keys_ref[2 * h + 1], q2, NT)
            e1, c, p = _peer_router(s1, s2)
            s2_scr[h] = s2
            p_scr[h] = p
            e1_scr[h] = e1
            c_scr[h] = c
            return carry

        lax.fori_loop(0, PEER_HEADS, route, 0)

    cur, prev = j % 2, (j + 1) % 2
    acc_scr[...] += _dot(vt_ref[...], g_scr[cur])
    hh = ht_scr[prev]
    g_scr[prev] = (w_scr[prev] * (0.5 * hh * (1.0 + lax.erf(hh * INV_SQRT2)))).astype(bf16)
    ht_scr[cur] = _dot(u_ref[...], xt_scr[...])
    jc = jnp.minimum(j, n_chunks - 1)
    n_rows = PEER_TE // PEER_NKEYS
    for rr in range(n_rows):
        r = jc * n_rows + rr
        w = jnp.zeros((PEER_NKEYS, tn), f32)
        for h in range(PEER_HEADS):
            sel = s2_scr[h] >= c_scr[h, pl.ds(r, 1), :]
            w = w + jnp.where(sel, p_scr[h], 0.0) * e1_scr[h, pl.ds(r, 1), :]
        w_scr[cur, rr * PEER_NKEYS:(rr + 1) * PEER_NKEYS, :] = w

    @pl.when(j == n_chunks + 1)
    def _():
        y = acc_scr[...].T
        o_ref[...] = _layer_norm(alpha * x_ref[...] + y, ln_ref[0:1, :], ln_ref[1:2, :])


def _peer_ffn(x, wq_hi, wq_lo, keys, u_bf, vt_bf, ln, alpha):
    n = x.shape[0]
    n_exp = u_bf.shape[0]
    assert n % PEER_TN == 0 and n_exp == PEER_NKEYS * PEER_NKEYS and n_exp % PEER_TE == 0
    n_chunks = n_exp // PEER_TE
    q = _peer_query(x, wq_hi, wq_lo)
    nq = q.shape[1]
    head_scr = pltpu.VMEM((PEER_HEADS, PEER_NKEYS, PEER_TN), f32)
    return pl.pallas_call(
        functools.partial(_peer_kernel, alpha, n_chunks),
        grid=(n // PEER_TN, n_chunks + 2),
        in_specs=[
            pl.BlockSpec((PEER_TN, D_MODEL), lambda i, j: (i, 0)),
            pl.BlockSpec((PEER_TN, nq), lambda i, j: (i, 0)),
            pl.BlockSpec((2 * PEER_HEADS, PEER_NKEYS, PEER_HALF), lambda i, j: (0, 0, 0)),
            pl.BlockSpec((PEER_TE, D_MODEL), lambda i, j: (jnp.minimum(j, n_chunks - 1), 0)),
            pl.BlockSpec((D_MODEL, PEER_TE), lambda i, j: (0, jnp.clip(j - 2, 0, n_chunks - 1))),
            pl.BlockSpec((2, D_MODEL), lambda i, j: (0, 0)),
        ],
        out_specs=pl.BlockSpec((PEER_TN, D_MODEL), lambda i, j: (i, 0)),
        out_shape=jax.ShapeDtypeStruct((n, D_MODEL), f32),
        scratch_shapes=[pltpu.VMEM((D_MODEL, PEER_TN), bf16), head_scr, head_scr, head_scr, head_scr,
                        pltpu.VMEM((D_MODEL, PEER_TN), f32), pltpu.VMEM((2, PEER_TE, PEER_TN), bf16),
                        pltpu.VMEM((2, PEER_TE, PEER_TN), f32), pltpu.VMEM((2, PEER_TE, PEER_TN), f32)],
        compiler_params=_cparams("parallel", "arbitrary"),
        name="peer_ffn",
    )(x, q, keys, u_bf, vt_bf, ln)


POOL_TM = 256


def kernel(x_prompt, x_sample, cache_k, cache_v, cache_logf, page_table, state_wkv, state_shift, state_pool,
           meta, w_in, b_f, w_o, mu_shift, w0, w2, a0, a2, g2, k_k, k_a, r_k, gn_g, gn_b, w_pool, pool_scale,
           ln_g, ln_b, peer_wq, peer_keys, peer_u, peer_v):
    prm = dict(mu_shift=mu_shift, w0=w0, w2=w2, a0=a0, a2=a2, g2=g2, k_k=k_k, k_a=k_a, r_k=r_k,
               gn_g=gn_g, gn_b=gn_b)
    bp, seq, d = x_prompt.shape
    db, dec_seq, _ = x_sample.shape
    assert bp == 1 and d == D_MODEL
    depth = ln_g.shape[0]
    alpha = (2 * depth) ** 0.25
    n_sample = db * dec_seq
    t_p = seq + N_META
    n_valid = n_sample + t_p
    n_rows = n_sample + -(-t_p // FOX_TQ) * FOX_TQ
    n_rows = -(-n_rows // ROW_TILE) * ROW_TILE
    h = jnp.concatenate([x_sample.reshape(n_sample, d), meta.astype(f32), x_prompt[0],
                         jnp.zeros((n_rows - n_valid, d), f32)], axis=0)

    ks, vs, lfs, wkvs, shifts, pools = ([[], []] for _ in range(6))
    n_qkv = A_IN + 3 * D_B
    for i in range(depth):
        j = i // 2
        ln1 = jnp.stack([ln_g[i, 0], ln_b[i, 0]])
        ln2 = jnp.stack([ln_g[i, 1], ln_b[i, 1]])
        if i % 2 == 0:
            w_cat = jnp.concatenate([w_in[j][:, :n_qkv],
                                     jnp.pad(w_in[j][:, n_qkv:], ((0, 0), (0, LANES - H_B)))], axis=1).astype(bf16)
            b_pad = jnp.pad(b_f[j], (0, LANES - H_B)).reshape(1, LANES)
            pa, kv, qkv16, lf = _in_projection(h, w_cat, b_pad)
            ya, wkv_p, wkv_s = _rwkv_mixer(pa, state_shift[j], state_wkv[j], prm, j, n_sample, n_valid, dec_seq)
            yb = _fox_attention(qkv16, kv, lf, cache_k[j], cache_v[j], cache_logf[j], page_table,
                                n_sample, n_valid, dec_seq)
            h1 = _out_projection(ya, yb, h, w_o[j].astype(bf16), ln1, alpha)
            for grp, lo, hi, lead in ((0, n_sample, n_valid, (bp, t_p)), (1, 0, n_sample, (db, dec_seq))):
                ks[grp].append(kv[lo:hi, :D_B].reshape(lead + (H_B, HEAD_DIM)))
                vs[grp].append(kv[lo:hi, D_B:].reshape(lead + (H_B, HEAD_DIM)))
                lfs[grp].append(lf[lo:hi, :H_B].reshape(lead + (H_B,)))
                shifts[grp].append(pa[lo:hi].reshape(lead + (A_IN,))[:, -1])
            wkvs[0].append(wkv_p)
            wkvs[1].append(wkv_s)
        else:
            ext_s = jnp.concatenate([jnp.zeros((db, POOL_HIST - POOL_BUF, d), f32), state_pool[j],
                                     h[:n_sample].reshape(db, dec_seq, d)], axis=1)
            pools[0].append(h[n_valid - POOL_BUF:n_valid][None])
            pools[1].append(ext_s[:, -POOL_BUF:])
            w_bf = w_pool[j].astype(bf16)
            scale = pool_scale[j].reshape(1, d)
            first = n_sample // POOL_TM
            hp = _pool_mixer(h, w_bf, scale, ln1, alpha, POOL_TM, first, n_rows // POOL_TM - first, True, n_sample)
            ext_len = ext_s.shape[1]
            hs = _pool_mixer(ext_s.reshape(db * ext_len, d), w_bf, scale, ln1, alpha, db * ext_len, 0, 1, False, 0)
            hs = hs.reshape(db, ext_len, d)[:, POOL_HIST:].reshape(n_sample, d)
            h1 = jnp.concatenate([hs, hp[n_sample:]], axis=0)
        wq = peer_wq[i]
        wq_hi = wq.astype(bf16)
        wq_lo = (wq - wq_hi.astype(f32)).astype(bf16)
        h = _peer_ffn(h1, wq_hi, wq_lo, peer_keys[i].reshape(2 * PEER_HEADS, PEER_NKEYS, PEER_HALF),
                      peer_u[i].astype(bf16), _transpose_bf16(peer_v[i]), ln2, alpha)

    y_prompt = h[n_sample + N_META:n_valid][None]
    y_sample = h[:n_sample].reshape(db, dec_seq, d)
    st = lambda xs: jnp.stack(xs)
    return (y_prompt, y_sample,
            st(ks[0]), st(vs[0]), st(lfs[0]), st(wkvs[0]), st(shifts[0]), st(pools[0]),
            st(ks[1]), st(vs[1]), st(lfs[1]), st(wkvs[1]), st(shifts[1]), st(pools[1]))
```

```python
import functools
import math

import jax
import jax.numpy as jnp
from jax import lax
from jax.experimental import pallas as pl
from jax.experimental.pallas import tpu as pltpu

f32 = jnp.float32
bf16 = jnp.bfloat16

D_MODEL = 1024
N_META = 16
HEAD_DIM = 64
D_A = D_MODEL // 2
D_B = D_MODEL - D_A
H_A = D_A // HEAD_DIM
H_B = D_B // HEAD_DIM
LORA_W = 64
LORA_A = 64
LORA_G = 128
A_K = D_A
A_V = 2 * D_A
A_W = 3 * D_A
A_IN = A_W + LORA_W + LORA_A + LORA_G
GN_EPS = HEAD_DIM * 1e-5
POOL_WINDOWS = (2, 4, 8, 16)
POOL_G = D_MODEL // len(POOL_WINDOWS)
POOL_BUF = max(POOL_WINDOWS) - 1
PEER_HEADS = 8
PEER_NKEYS = 128
PEER_TOPK = 16
PEER_HALF = 128
LN_EPS = 1e-5
NEG = -1e30

LANES = 128
PAIR = 2 * HEAD_DIM
N_PAIR = D_A // PAIR
ROW_TILE = 512
SCAN_C = 64
INV_BLOCK = 16
FOX_TQ = 256
FOX_TK = 256
PAGES_PER_STEP = 8
PEER_TN = 256
PEER_TE = 1024
VMEM_LIMIT = 56 * 1024 * 1024


def _cparams(*sem):
    return pltpu.CompilerParams(dimension_semantics=sem, vmem_limit_bytes=VMEM_LIMIT)


def _dot(a, b, dims=(((1,), (0,)), ((), ()))):
    return lax.dot_general(a, b, dims, preferred_element_type=f32)


NT = (((1,), (1,)), ((), ()))
TN = (((0,), (0,)), ((), ()))
NN = (((1,), (0,)), ((), ()))


def _split(x):
    hi = x.astype(bf16)
    lo = (x - hi.astype(f32)).astype(bf16)
    return hi, lo


def _dot3(a, b, dims=NN):
    ah, al = _split(a)
    bh, bl = _split(b)
    return _dot(ah, bh, dims) + (_dot(al, bh, dims) + _dot(ah, bl, dims))


def _dot_exact_rhs(a, b_bf, dims=NN):
    a0 = a.astype(bf16)
    r1 = a - a0.astype(f32)
    a1 = r1.astype(bf16)
    a2 = (r1 - a1.astype(f32)).astype(bf16)
    return _dot(a0, b_bf, dims) + (_dot(a1, b_bf, dims) + _dot(a2, b_bf, dims))


def _dot_exact_lhs(a_bf, b, dims=NN):
    b0 = b.astype(bf16)
    r1 = b - b0.astype(f32)
    b1 = r1.astype(bf16)
    b2 = (r1 - b1.astype(f32)).astype(bf16)
    return _dot(a_bf, b0, dims) + (_dot(a_bf, b1, dims) + _dot(a_bf, b2, dims))


def _layer_norm(x, g, b):
    mu = jnp.mean(x, axis=-1, keepdims=True)
    xc = x - mu
    var = jnp.mean(xc * xc, axis=-1, keepdims=True)
    return xc * lax.rsqrt(var + LN_EPS) * g + b


def _head_ones():
    r = lax.broadcasted_iota(jnp.int32, (D_A, D_A), 0) // HEAD_DIM
    c = lax.broadcasted_iota(jnp.int32, (D_A, D_A), 1) // HEAD_DIM
    return jnp.where(r == c, 1.0, 0.0).astype(bf16)


def _proj_kernel(x_ref, w_ref, bf_ref, pa_ref, kv_ref, qkv16_ref, lf_ref):
    x = x_ref[...].astype(bf16)
    y = _dot(x, w_ref[...])
    pa_ref[...] = y[:, :A_IN]
    q = y[:, A_IN:A_IN + D_B]
    kv = y[:, A_IN + D_B:A_IN + 3 * D_B]
    kv_ref[...] = kv
    qkv16_ref[:, :D_B] = (q * (HEAD_DIM ** -0.5)).astype(bf16)
    qkv16_ref[:, D_B:] = kv.astype(bf16)
    lf_ref[...] = jax.nn.log_sigmoid(y[:, A_IN + 3 * D_B:] + bf_ref[...])


def _in_projection(x, w_bf, b_f_pad):
    n = x.shape[0]
    nw = w_bf.shape[1]
    return pl.pallas_call(
        _proj_kernel,
        grid=(n // ROW_TILE,),
        in_specs=[
            pl.BlockSpec((ROW_TILE, D_MODEL), lambda i: (i, 0)),
            pl.BlockSpec((D_MODEL, nw), lambda i: (0, 0)),
            pl.BlockSpec((1, LANES), lambda i: (0, 0)),
        ],
        out_specs=[
            pl.BlockSpec((ROW_TILE, A_IN), lambda i: (i, 0)),
            pl.BlockSpec((ROW_TILE, 2 * D_B), lambda i: (i, 0)),
            pl.BlockSpec((ROW_TILE, 3 * D_B), lambda i: (i, 0)),
            pl.BlockSpec((ROW_TILE, LANES), lambda i: (i, 0)),
        ],
        out_shape=[
            jax.ShapeDtypeStruct((n, A_IN), f32),
            jax.ShapeDtypeStruct((n, 2 * D_B), f32),
            jax.ShapeDtypeStruct((n, 3 * D_B), bf16),
            jax.ShapeDtypeStruct((n, LANES), f32),
        ],
        compiler_params=_cparams("parallel"),
        name="in_projection",
    )(x, w_bf, b_f_pad)


PREP_TM = 256


def _prep_kernel(n_sample, n_valid, dec_seq,
                 pa_ref, prev_ref, ss_ref, mu_ref, vec_ref, w2_ref, a2_ref, g2_ref,
                 r_o, k_o, v_o, kk_o, b_o, lw_o, g_o, bonus_o):
    i = pl.program_id(0)
    tm = pa_ref.shape[0]
    pa = pa_ref[...]
    local = lax.broadcasted_iota(jnp.int32, (tm, 1), 0)
    row = i * tm + local
    prev = pltpu.roll(pa, 1, 0)
    prev = jnp.where(local == 0, prev_ref[7:8, :], prev)
    prev = jnp.where(row == n_sample, 0.0, prev)
    seq_start = jnp.logical_and(row < n_sample, row % dec_seq == 0)
    prev = jnp.where(seq_start, ss_ref[...], prev)
    xs = pa + (prev - pa) * mu_ref[...]

    w0 = vec_ref[0:1, :]
    a0 = vec_ref[1:2, :]
    k_k = vec_ref[2:3, :]
    k_a = vec_ref[3:4, :]
    r_k = vec_ref[4:5, :]
    r = xs[:, :A_K]
    k = xs[:, A_K:A_V]
    v = xs[:, A_V:A_W]
    wa = xs[:, A_W:A_W + LORA_W + LORA_A]
    xg = xs[:, A_W + LORA_W + LORA_A:]
    w = -jax.nn.softplus(-(w0 + _dot3(jnp.tanh(wa), w2_ref[...]))) - 0.5
    lw = -jnp.exp(w)
    a = jax.nn.sigmoid(a0 + _dot3(wa, a2_ref[...]))
    g = _dot(jax.nn.sigmoid(xg).astype(bf16), g2_ref[...].astype(bf16))
    ones = _head_ones()
    kk = k * k_k
    kk = kk / jnp.maximum(jnp.sqrt(_dot_exact_rhs(kk * kk, ones)), 1e-12)
    k = k * (1.0 + (a - 1.0) * k_a)
    bonus = _dot_exact_rhs(r * k * r_k, ones) * v
    valid = row < n_valid
    r_o[...] = r
    k_o[...] = jnp.where(valid, k, 0.0)
    v_o[...] = jnp.where(valid, v, 0.0)
    kk_o[...] = kk
    b_o[...] = jnp.where(valid, kk * a, 0.0)
    lw_o[...] = jnp.where(valid, lw, 0.0)
    g_o[...] = g
    bonus_o[...] = bonus


def _rwkv_prep(pa, ss_rows, mu, vecs, w2p, a2p, g2, n_sample, n_valid, dec_seq):
    n = pa.shape[0]
    assert n_sample == PREP_TM and n % PREP_TM == 0
    row_spec = pl.BlockSpec((PREP_TM, D_A), lambda i: (i, 0))
    const = lambda shape: pl.BlockSpec(shape, lambda i: (0, 0))
    return pl.pallas_call(
        functools.partial(_prep_kernel, n_sample, n_valid, dec_seq),
        grid=(n // PREP_TM,),
        in_specs=[
            pl.BlockSpec((PREP_TM, A_IN), lambda i: (i, 0)),
            pl.BlockSpec((8, A_IN), lambda i: (jnp.maximum(i * (PREP_TM // 8) - 1, 0), 0)),
            const((PREP_TM, A_IN)),
            const((1, A_IN)),
            const((8, D_A)),
            const((LORA_W + LORA_A, D_A)),
            const((LORA_W + LORA_A, D_A)),
            const((LORA_G, D_A)),
        ],
        out_specs=[row_spec] * 8,
        out_shape=[jax.ShapeDtypeStruct((n, D_A), f32)] * 8,
        compiler_params=_cparams("parallel"),
        name="rwkv_prep",
    )(pa, pa, ss_rows, mu, vecs, w2p, a2p, g2)


def _stack_heads(x, lane_a):
    return jnp.concatenate([jnp.where(lane_a, x, 0.0), jnp.where(lane_a, 0.0, x)], axis=0)


def _unit_lower_inverse(x):
    n = x.shape[0]
    ri = lax.broadcasted_iota(jnp.int32, (n, n), 0)
    ci = lax.broadcasted_iota(jnp.int32, (n, n), 1)
    eye = jnp.where(ri == ci, 1.0, 0.0)
    d = jnp.where(ri // INV_BLOCK == ci // INV_BLOCK, x, 0.0)
    off = x - d
    d2 = _dot3(d, d)
    yield
    d4 = _dot3(d2, d2)
    yield
    d8 = _dot3(d4, d4)
    p = eye - d
    p = p + _dot3(p, d2)
    yield
    p = p + _dot3(p, d4)
    yield
    dinv = p + _dot3(p, d8)
    yield
    m = _dot3(dinv, off)
    yield
    m2 = _dot3(m, m)
    yield
    q = eye - m
    q = q + _dot3(q, m2)
    yield
    return _dot3(q, dinv)


def _scan_pair(r, k, v, kk, b, lw, g, bonus, gn, S):
    C = r.shape[0]
    assert C == SCAN_C and SCAN_C // INV_BLOCK == 4 and 2 * C == PAIR
    lane = lax.broadcasted_iota(jnp.int32, (1, PAIR), 1)
    lane_a = lane < HEAD_DIM
    t_idx = lax.broadcasted_iota(jnp.int32, (C, 1), 0)
    tri = jnp.where(lax.broadcasted_iota(jnp.int32, (C, C), 1) <= t_idx, 1.0, 0.0).astype(bf16)
    cs = _dot_exact_lhs(tri, lw)
    total = cs[C - 1:C, :]
    kp = kk * jnp.exp(cs - lw)
    rp = r * jnp.exp(cs)
    g_inv = jnp.exp(-cs)
    g_tail = jnp.exp(total - cs)
    kt, bt = k * g_inv, b * g_inv
    kh, bh = k * g_tail, b * g_tail

    st = lambda x: _stack_heads(x, lane_a)
    sc = _dot3(jnp.concatenate([kp, rp], axis=0),
               jnp.concatenate([st(bt), st(kt)], axis=0), NT)
    yield
    s_idx = lane % C
    strict = s_idx < t_idx
    incl = s_idx <= t_idx
    lb = jnp.where(strict, sc[:C, :PAIR], 0.0)
    lk = jnp.where(strict, sc[:C, PAIR:], 0.0)
    pb = jnp.where(incl, sc[C:, :PAIR], 0.0)
    pk = jnp.where(incl, sc[C:, PAIR:], 0.0)

    lkv = _dot3(lk, st(v))
    yield
    tinv = yield from _unit_lower_inverse(st(lb))
    yield
    tcat = tinv[:C] + tinv[C:]
    w12 = _dot3(tcat, jnp.concatenate([st(lkv), st(kp)], axis=1))
    yield
    w1, w2 = w12[:, :PAIR], w12[:, PAIR:]
    pbw = _dot(pb.astype(bf16), jnp.concatenate([st(w1), st(w2)], axis=1).astype(bf16))
    pkv = _dot(pk.astype(bf16), st(v).astype(bf16))
    yield
    q2 = rp - pbw[:, PAIR:]
    y = _dot(q2.astype(bf16), S.astype(bf16), NT) + (pkv - pbw[:, :PAIR])

    ri = lax.broadcasted_iota(jnp.int32, (PAIR, PAIR), 0)
    ci = lax.broadcasted_iota(jnp.int32, (PAIR, PAIR), 1)
    same_head = (ri // HEAD_DIM) == (ci // HEAD_DIM)
    wb = _dot3(jnp.concatenate([w2, w1], axis=1), bh, TN)
    yield
    g2m = jnp.where(ri == ci, jnp.exp(total), 0.0) - jnp.where(same_head, wb[:PAIR], 0.0)
    g1m = jnp.where(same_head, _dot3(v, kh, TN) - wb[PAIR:], 0.0)
    yield
    s_new = _dot3(S, g2m) + g1m
    yield

    ones = jnp.where(same_head, 1.0, 0.0).astype(bf16)
    mean = _dot_exact_rhs(y, ones) * (1.0 / HEAD_DIM)
    yc = y - mean
    var = _dot_exact_rhs(yc * yc, ones) * (1.0 / HEAD_DIM)
    yn = yc * lax.rsqrt(var + GN_EPS) * gn[0:1, :] + gn[1:2, :]
    return (yn + bonus) * g, s_new


def _scan_kernel(per_chunk_state, *refs):
    if per_chunk_state:
        (r_ref, k_ref, v_ref, kk_ref, b_ref, lw_ref, g_ref, bonus_ref, gn_ref, s0_ref,
         ya_ref, sout_ref, s_scr) = refs
        s_scr[...] = s0_ref[0]
    else:
        (r_ref, k_ref, v_ref, kk_ref, b_ref, lw_ref, g_ref, bonus_ref, gn_ref,
         ya_ref, sout_ref, s_scr) = refs

        @pl.when(pl.program_id(0) == 0)
        def _():
            s_scr[...] = jnp.zeros(s_scr.shape, f32)

    in_refs = (r_ref, k_ref, v_ref, kk_ref, b_ref, lw_ref, g_ref, bonus_ref, gn_ref)
    lanes = [slice(p * PAIR, (p + 1) * PAIR) for p in range(N_PAIR)]
    chains = [_scan_pair(*(x[:, sl] for x in in_refs), s_scr[p]) for p, sl in enumerate(lanes)]
    pending = list(range(N_PAIR))
    while pending:
        for p in list(pending):
            try:
                next(chains[p])
            except StopIteration as done:
                ya, s_new = done.value
                ya_ref[:, lanes[p]] = ya
                s_scr[p] = s_new
                sout_ref[0, p] = s_new
                pending.remove(p)


def _rwkv_scan(arrs, gn, n_rows_out, row_block_off, n_chunks, s0=None):
    per_chunk = s0 is not None
    blk = pl.BlockSpec((SCAN_C, D_A), lambda c: (c + row_block_off, 0))
    in_specs = [blk] * 8 + [pl.BlockSpec((2, D_A), lambda c: (0, 0))]
    args = list(arrs) + [gn]
    if per_chunk:
        in_specs.append(pl.BlockSpec((1, N_PAIR, PAIR, PAIR), lambda c: (c, 0, 0, 0)))
        args.append(s0)
        n_states = n_chunks
        s_map = lambda c: (c, 0, 0, 0)
    else:
        n_states = 1
        s_map = lambda c: (0, 0, 0, 0)
    return pl.pallas_call(
        functools.partial(_scan_kernel, per_chunk),
        grid=(n_chunks,),
        in_specs=in_specs,
        out_specs=[blk, pl.BlockSpec((1, N_PAIR, PAIR, PAIR), s_map)],
        out_shape=[jax.ShapeDtypeStruct((n_rows_out, D_A), f32),
                   jax.ShapeDtypeStruct((n_states, N_PAIR, PAIR, PAIR), f32)],
        scratch_shapes=[pltpu.VMEM((N_PAIR, PAIR, PAIR), f32)],
        compiler_params=_cparams("arbitrary"),
        name="rwkv_scan",
    )(*args)


def _pair_states(state):
    b = state.shape[0]
    s = state.reshape(b, N_PAIR, 2, HEAD_DIM, HEAD_DIM)
    out = jnp.zeros((b, N_PAIR, 2, HEAD_DIM, 2, HEAD_DIM), f32)
    out = out.at[:, :, 0, :, 0, :].set(s[:, :, 0]).at[:, :, 1, :, 1, :].set(s[:, :, 1])
    return out.reshape(b, N_PAIR, PAIR, PAIR)


def _unpair_states(s):
    b = s.shape[0]
    s = s.reshape(b, N_PAIR, 2, HEAD_DIM, 2, HEAD_DIM)
    return jnp.stack([s[:, :, 0, :, 0, :], s[:, :, 1, :, 1, :]], axis=2).reshape(b, H_A, HEAD_DIM, HEAD_DIM)


def _rwkv_mixer(pa, state_shift, state_wkv, prm, j, n_sample, n_valid, dec_seq):
    n = pa.shape[0]
    db = n_sample // dec_seq
    ss_rows = jnp.zeros((db, dec_seq, A_IN), f32).at[:, 0].set(state_shift).reshape(n_sample, A_IN)
    vecs = jnp.zeros((8, D_A), f32)
    for idx, name in enumerate(("w0", "a0", "k_k", "k_a", "r_k")):
        vecs = vecs.at[idx].set(prm[name][j].reshape(D_A))
    zpad = jnp.zeros((LORA_W, D_A), f32)
    w2p = jnp.concatenate([prm["w2"][j], zpad], axis=0)
    a2p = jnp.concatenate([zpad, prm["a2"][j]], axis=0)
    arrs = _rwkv_prep(pa, ss_rows, prm["mu_shift"][j].reshape(1, A_IN), vecs, w2p, a2p, prm["g2"][j],
                      n_sample, n_valid, dec_seq)
    gn = jnp.stack([prm["gn_g"][j], prm["gn_b"][j]])
    n_chunks = -(-(n_valid - n_sample) // SCAN_C)
    assert n_sample % SCAN_C == 0 and n_sample + n_chunks * SCAN_C <= n and dec_seq <= SCAN_C
    ya_p, s_p = _rwkv_scan(arrs, gn, n, n_sample // SCAN_C, n_chunks)

    def pad_seq(x):
        x = x[:n_sample].reshape(db, dec_seq, D_A)
        return jnp.pad(x, ((0, 0), (0, SCAN_C - dec_seq), (0, 0))).reshape(db * SCAN_C, D_A)

    ya_s, s_s = _rwkv_scan([pad_seq(x) for x in arrs], gn, db * SCAN_C, 0, db, s0=_pair_states(state_wkv))
    ya_s = ya_s.reshape(db, SCAN_C, D_A)[:, :dec_seq].reshape(n_sample, D_A)
    ya = jnp.concatenate([ya_s, ya_p[n_sample:]], axis=0)
    return ya, _unpair_states(s_p[0][None]), _unpair_states(s_s)


CUM_BLOCK = 256


def _upper_ones(n):
    r = lax.broadcasted_iota(jnp.int32, (n, n), 0)
    c = lax.broadcasted_iota(jnp.int32, (n, n), 1)
    return jnp.where(r <= c, 1.0, 0.0).astype(bf16)


def _cumsum_kernel(x_ref, d_ref, c_ref, carry):
    @pl.when(pl.program_id(0) == 0)
    def _():
        carry[...] = jnp.zeros_like(carry)

    x = x_ref[...]
    n = x.shape[1]
    d_ref[...] = _dot_exact_rhs(x, _upper_ones(n))
    c_ref[0] = carry[...]
    carry[...] = carry[...] + _dot_exact_rhs(x, jnp.ones((n, LANES), bf16))


def _logf_cumsum(lf_t, first_block):
    h, n = lf_t.shape
    nb = n // CUM_BLOCK
    return pl.pallas_call(
        _cumsum_kernel,
        grid=(nb - first_block,),
        in_specs=[pl.BlockSpec((h, CUM_BLOCK), lambda i: (0, i + first_block))],
        out_specs=[pl.BlockSpec((h, CUM_BLOCK), lambda i: (0, i + first_block)),
                   pl.BlockSpec((1, h, LANES), lambda i: (i + first_block, 0, 0))],
        out_shape=[jax.ShapeDtypeStruct((h, n), f32), jax.ShapeDtypeStruct((nb, h, LANES), f32)],
        scratch_shapes=[pltpu.VMEM((h, LANES), f32)],
        compiler_params=_cparams("arbitrary"),
        name="logf_cumsum",
    )(lf_t)


FOX_KW = 2 * PAIR


def _fox_prompt_kernel(row0, cref_ref, q_ref, k_ref, va_ref, vb_ref, o_ref):
    pair = pl.program_id(0)
    i = pl.program_id(1)
    q = q_ref[...]
    lane = lax.broadcasted_iota(jnp.int32, (FOX_TQ, PAIR), 1)
    lane_a = lane < HEAD_DIM
    zero = jnp.zeros_like(q)

    def query_t(qh, col):
        bias = jnp.where(jnp.logical_or(lane == col, lane == col + 1), -1.0, 0.0)
        qa = jnp.concatenate([qh.astype(f32), bias], axis=1)
        return qa.T.astype(bf16)

    q_t = (query_t(jnp.where(lane_a, q, zero), 0), query_t(jnp.where(lane_a, zero, q), 2))
    v_refs = (va_ref, vb_ref)
    key_idx = lax.broadcasted_iota(jnp.int32, (FOX_TK, 1), 0)
    qry_idx = lax.broadcasted_iota(jnp.int32, (1, FOX_TQ), 1)
    blk0 = row0 // FOX_TK

    def block_start(j):
        return pl.multiple_of(row0 + j * FOX_TK, FOX_TK)

    def scores(j):
        kb = k_ref[pl.ds(block_start(j), FOX_TK), :]
        return tuple(_dot(kb, q_t[hh]) for hh in range(2))

    def update(j, s_pair, carry, masked):
        ps, ms, alphas = [], [], []
        for hh in range(2):
            m = carry[hh][0]
            s = s_pair[hh]
            if masked:
                s = jnp.where(key_idx <= qry_idx, s, NEG)
            cref = cref_ref[(blk0 + j) * H_B + 2 * pair + hh]
            m_new = jnp.maximum(m, jnp.max(s, axis=0, keepdims=True) - cref)
            ps.append(jnp.exp((s - (m_new + cref)).astype(bf16)))
            ms.append(m_new)
            alphas.append(jnp.exp(m - m_new))
        start = block_start(j)
        pvs = [_dot(v_refs[hh][pl.ds(start, FOX_TK), :], ps[hh], TN) for hh in range(2)]
        return tuple((ms[hh], carry[hh][1] * alphas[hh] + pvs[hh]) for hh in range(2))

    def step(j, state):
        carry, s_cur = state
        s_next = scores(j + 1)
        return update(j, s_cur, carry, False), s_next

    init = tuple((jnp.full((1, FOX_TQ), NEG, f32), jnp.zeros((PAIR, FOX_TQ), f32)) for _ in range(2))
    carry, s_last = lax.fori_loop(0, i, step, (init, scores(0)))
    (_, acc_a), (_, acc_b) = update(i, s_last, carry, True)
    out_t = jnp.concatenate([acc_a[:HEAD_DIM] / acc_a[HEAD_DIM:HEAD_DIM + 1],
                             acc_b[HEAD_DIM:] / acc_b[0:1]], axis=0)
    o_ref[...] = out_t.T


def _fox_prompt(q16, k_aug, v_a, v_b, cref, n_sample, n_valid):
    n = q16.shape[0]
    assert FOX_TQ == FOX_TK == CUM_BLOCK and n_sample % FOX_TQ == 0
    nq = -(-(n_valid - n_sample) // FOX_TQ)
    assert n_sample + nq * FOX_TQ <= n
    qb0 = n_sample // FOX_TQ
    nb = D_B // PAIR
    return pl.pallas_call(
        functools.partial(_fox_prompt_kernel, n_sample),
        grid=(nb, nq),
        in_specs=[
            pl.BlockSpec(memory_space=pltpu.SMEM),
            pl.BlockSpec((FOX_TQ, PAIR), lambda p, i: (i + qb0, p)),
            pl.BlockSpec((n, FOX_KW), lambda p, i: (0, p)),
            pl.BlockSpec((n, PAIR), lambda p, i: (0, p)),
            pl.BlockSpec((n, PAIR), lambda p, i: (0, p)),
        ],
        out_specs=pl.BlockSpec((FOX_TQ, PAIR), lambda p, i: (i + qb0, p)),
        out_shape=jax.ShapeDtypeStruct((n, D_B), f32),
        compiler_params=_cparams("parallel", "arbitrary"),
        name="fox_prompt",
    )(cref, q16, k_aug, v_a, v_b)


def _fox_sample_kernel(n_steps, page_size, ds, pt_ref, q_ref, kn_ref, vn_ref, lfn_ref, *refs):
    np_ = PAGES_PER_STEP
    k_refs, v_refs, lf_refs = refs[:np_], refs[np_:2 * np_], refs[2 * np_:3 * np_]
    o_ref, m_scr, l_scr, acc_scr, carry_scr, z_scr = refs[3 * np_:]
    j = pl.program_id(1)
    nq = H_B * ds

    @pl.when(j == 0)
    def _():
        m_scr[...] = jnp.full(m_scr.shape, NEG, f32)
        l_scr[...] = jnp.zeros(l_scr.shape, f32)
        acc_scr[...] = jnp.zeros(acc_scr.shape, f32)
        carry_scr[...] = jnp.zeros(carry_scr.shape, f32)

    qm = q_ref[0].astype(bf16)
    lane_head = lax.broadcasted_iota(jnp.int32, (1, nq), 1) // ds
    expand =jnp.where(lax.broadcasted_iota(jnp.int32, (LANES, nq), 0)
                       == lax.broadcasted_iota(jnp.int32, (LANES, nq), 1) // ds, 1.0, 0.0).astype(bf16)
    t_idx = lax.broadcasted_iota(jnp.int32, (page_size, 1), 0)
    lower = jnp.where(lax.broadcasted_iota(jnp.int32, (page_size, page_size), 1) <= t_idx, 1.0, 0.0).astype(bf16)
    lane_pad = jnp.zeros((page_size, LANES - H_B), f32)

    def attend(k3s, v3s, lfs, mask):
        run = carry_scr[...]
        nk = page_size * len(k3s)
        rows = nk * H_B
        biases = []
        for i, (k3, lf) in enumerate(zip(k3s, lfs)):
            z_scr[i * page_size * H_B:(i + 1) * page_size * H_B, :] = _dot(
                k3.reshape(page_size * H_B, HEAD_DIM).astype(bf16), qm, NT)
            lfe = _dot_exact_rhs(jnp.concatenate([lf, lane_pad], axis=1), expand)
            biases.append(_dot_exact_lhs(lower, lfe) + run)
            run = run + jnp.sum(lfe, axis=0, keepdims=True)
        carry_scr[...] = run
        s = z_scr[pl.ds(0, nk, stride=H_B), :]
        for h in range(1, H_B):
            s = jnp.where(lane_head == h, z_scr[pl.ds(h, nk, stride=H_B), :], s)
        s = s - (biases[0] if len(biases) == 1 else jnp.concatenate(biases, axis=0))
        if mask is not None:
            s = jnp.where(mask, s, NEG)
        m = m_scr[...]
        m_new = jnp.maximum(m, jnp.max(s, axis=0, keepdims=True))
        alpha = jnp.exp(m - m_new)
        p = jnp.exp(s - m_new)
        l_scr[...] = l_scr[...] * alpha + jnp.sum(p, axis=0, keepdims=True)
        for h in range(H_B):
            z_scr[pl.ds(h, nk, stride=H_B), :] = jnp.where(lane_head == h, p, 0.0)
        v2 = [v3.reshape(page_size * H_B, HEAD_DIM) for v3 in v3s]
        v2 = (v2[0] if len(v2) == 1 else jnp.concatenate(v2, axis=0)).astype(bf16)
        acc_scr[...] = acc_scr[...] * alpha + _dot(v2, z_scr[0:rows, :].astype(bf16), TN)
        m_scr[...] = m_new

    attend([r[0] for r in k_refs], [r[0] for r in v_refs], [r[0] for r in lf_refs], None)

    @pl.when(j == n_steps - 1)
    def _():
        zk = jnp.zeros((page_size - ds, H_B, HEAD_DIM), f32)
        new3 = lambda ref: jnp.concatenate([ref[0].reshape(ds, H_B, HEAD_DIM), zk], axis=0)
        lfn = jnp.concatenate([lfn_ref[0], jnp.zeros((page_size - ds, H_B), f32)], axis=0)
        t_q = lax.broadcasted_iota(jnp.int32, (1, nq), 1) % ds
        attend([new3(kn_ref)], [new3(vn_ref)], [lfn], t_idx <= t_q)
        o_ref[0] = acc_scr[...] / l_scr[...]


def _fox_sample(q_m, k_new, v_new, lf_new, cache_k, cache_v, cache_lf, page_table, dec_seq):
    db = q_m.shape[0]
    n_pages = page_table.shape[1]
    page_size = cache_k.shape[1]
    assert n_pages % PAGES_PER_STEP == 0 and dec_seq <= page_size and H_B <= LANES
    n_steps = n_pages // PAGES_PER_STEP
    nq = H_B * dec_seq

    def page_map(pp, nd):
        return lambda b, j, pt: (pt[b, j * PAGES_PER_STEP + pp],) + (0,) * nd

    batch3 = lambda shape: pl.BlockSpec((1,) + shape, lambda b, j, pt: (b, 0, 0))
    in_specs = [batch3((nq, HEAD_DIM)), batch3((nq, HEAD_DIM)), batch3((nq, HEAD_DIM)), batch3((dec_seq, H_B))]
    in_specs += [pl.BlockSpec((1, page_size, H_B, HEAD_DIM), page_map(pp, 3)) for pp in range(PAGES_PER_STEP)]
    in_specs += [pl.BlockSpec((1, page_size, H_B, HEAD_DIM), page_map(pp, 3)) for pp in range(PAGES_PER_STEP)]
    in_specs += [pl.BlockSpec((1, page_size, H_B), page_map(pp, 2)) for pp in range(PAGES_PER_STEP)]
    grid_spec = pltpu.PrefetchScalarGridSpec(
        num_scalar_prefetch=1,
        grid=(db, n_steps),
        in_specs=in_specs,
        out_specs=batch3((HEAD_DIM, nq)),
        scratch_shapes=[pltpu.VMEM((1, nq), f32), pltpu.VMEM((1, nq), f32),
                        pltpu.VMEM((HEAD_DIM, nq), f32), pltpu.VMEM((1, nq), f32),
                        pltpu.VMEM((PAGES_PER_STEP * page_size * H_B, nq), f32)],
    )
    return pl.pallas_call(
        functools.partial(_fox_sample_kernel, n_steps, page_size, dec_seq),
        grid_spec=grid_spec,
        out_shape=jax.ShapeDtypeStruct((db, HEAD_DIM, nq), f32),
        compiler_params=_cparams("parallel", "arbitrary"),
        name="fox_sample",
    )(page_table, q_m, k_new, v_new, lf_new, *([cache_k] * PAGES_PER_STEP), *([cache_v] * PAGES_PER_STEP),
      *([cache_lf] * PAGES_PER_STEP))


def _fox_attention(qkv16, kv, lf, cache_k, cache_v, cache_lf, page_table, n_sample, n_valid, dec_seq):
    n = qkv16.shape[0]
    db = n_sample // dec_seq
    lf_t = lf[:, :H_B].T
    delta_t, c_blk = _logf_cumsum(lf_t, n_sample // CUM_BLOCK)
    delta = delta_t.T
    d_hi = delta.astype(bf16)
    d_lo = (delta - d_hi.astype(f32)).astype(bf16)
    k16, v16 = qkv16[:, D_B:2 * D_B], qkv16[:, 2 * D_B:]
    zpad = jnp.zeros((n, FOX_KW - PAIR - 4), bf16)
    parts = []
    for p in range(D_B // PAIR):
        parts += [k16[:, p * PAIR:(p + 1) * PAIR], d_hi[:, 2 * p:2 * p + 1], d_lo[:, 2 * p:2 * p + 1],
                  d_hi[:, 2 * p + 1:2 * p + 2], d_lo[:, 2 * p + 1:2 * p + 2], zpad]
    k_aug = jnp.concatenate(parts, axis=1)
    lane_a = (jnp.arange(D_B) % PAIR) < HEAD_DIM
    one = jnp.ones((), bf16)
    yb_p = _fox_prompt(qkv16, k_aug, jnp.where(lane_a, v16, one), jnp.where(lane_a, one, v16),
                       c_blk[:, :, 0].reshape(-1), n_sample, n_valid)
    nq = H_B * dec_seq
    q_m = qkv16[:n_sample, :D_B].astype(f32).reshape(db, dec_seq, H_B, HEAD_DIM)
    q_m = jnp.swapaxes(q_m, 1, 2).reshape(db, nq, HEAD_DIM)
    k_new = kv[:n_sample, :D_B].reshape(db, nq, HEAD_DIM)
    v_new = kv[:n_sample, D_B:].reshape(db, nq, HEAD_DIM)
    lf_new = lf[:n_sample, :H_B].reshape(db, dec_seq, H_B)
    o = _fox_sample(q_m, k_new, v_new, lf_new, cache_k, cache_v, cache_lf, page_table, dec_seq)
    yb_s = jnp.transpose(o.reshape(db, HEAD_DIM, H_B, dec_seq), (0, 3, 2, 1)).reshape(n_sample, D_B)
    return jnp.concatenate([yb_s, yb_p[n_sample:]], axis=0)


def _outproj_kernel(alpha, ya_ref, yb_ref, h_ref, w_ref, ln_ref, o_ref):
    y = _dot(ya_ref[...].astype(bf16), w_ref[:D_A, :]) + _dot(yb_ref[...].astype(bf16), w_ref[D_A:, :])
    o_ref[...] = _layer_norm(alpha * h_ref[...] + y, ln_ref[0:1, :], ln_ref[1:2, :])


def _out_projection(ya, yb, h, w_bf, ln, alpha):
    n = h.shape[0]
    half = pl.BlockSpec((ROW_TILE, D_A), lambda i: (i, 0))
    full = pl.BlockSpec((ROW_TILE, D_MODEL), lambda i: (i, 0))
    return pl.pallas_call(
        functools.partial(_outproj_kernel, alpha),
        grid=(n // ROW_TILE,),
        in_specs=[half, half, full,
                  pl.BlockSpec((D_MODEL, D_MODEL), lambda i: (0, 0)),
                  pl.BlockSpec((2, D_MODEL), lambda i: (0, 0))],
        out_specs=full,
        out_shape=jax.ShapeDtypeStruct((n, D_MODEL), f32),
        compiler_params=_cparams("parallel"),
        name="out_projection",
    )(ya, yb, h, w_bf, ln)


POOL_HIST = POOL_BUF + 1


def _pool_kernel(prompt_mode, zero_hist_tile, row0, alpha, x_ref, hist_ref, w_ref, sc_ref, ln_ref, o_ref):
    i = pl.program_id(0) + zero_hist_tile
    tm = x_ref.shape[0]
    x = x_ref[...]
    hist = jnp.where(i == zero_hist_tile, 0.0, hist_ref[...])
    ext = jnp.concatenate([hist, x], axis=0)
    t = i * tm + lax.broadcasted_iota(jnp.int32, (tm, 1), 0) - row0
    ys = []
    for gi, win in enumerate(POOL_WINDOWS):
        s = ext[:, gi * POOL_G:(gi + 1) * POOL_G]
        span = 1
        while span < win:
            s = s + pltpu.roll(s, span, 0)
            span *= 2
        cnt = jnp.minimum(win, t + 1).astype(f32) if prompt_mode else float(win)
        diff = s[POOL_HIST:] / cnt - x[:, gi * POOL_G:(gi + 1) * POOL_G]
        ys.append(_dot(diff.astype(bf16), w_ref[gi]))
    y = jnp.concatenate(ys, axis=1) * sc_ref[...]
    o_ref[...] = _layer_norm(alpha * x + y, ln_ref[0:1, :], ln_ref[1:2, :])


def _pool_mixer(x, w_bf, scale, ln, alpha, tm, first_tile, n_tiles, prompt_mode, row0):
    n = x.shape[0]
    assert tm % POOL_HIST == 0
    hb = tm // POOL_HIST
    return pl.pallas_call(
        functools.partial(_pool_kernel, prompt_mode, first_tile, row0, alpha),
        grid=(n_tiles,),
        in_specs=[
            pl.BlockSpec((tm, D_MODEL), lambda i: (i + first_tile, 0)),
            pl.BlockSpec((POOL_HIST, D_MODEL), lambda i: (jnp.maximum((i + first_tile) * hb - 1, 0), 0)),
            pl.BlockSpec((len(POOL_WINDOWS), POOL_G, POOL_G), lambda i: (0, 0, 0)),
            pl.BlockSpec((1, D_MODEL), lambda i: (0, 0)),
            pl.BlockSpec((2, D_MODEL), lambda i: (0, 0)),
        ],
        out_specs=pl.BlockSpec((tm, D_MODEL), lambda i: (i + first_tile, 0)),
        out_shape=jax.ShapeDtypeStruct((n, D_MODEL), f32),
        compiler_params=_cparams("parallel"),
        name="pool_mixer",
    )(x, x, w_bf, scale, ln)


def _peer_query_kernel(x_ref, wh_ref, wl_ref, o_ref):
    xh, xl = _split(x_ref[...])
    o_ref[...] = _dot(xh, wh_ref[...]) + (_dot(xl, wh_ref[...]) + _dot(xh, wl_ref[...]))


def _peer_query(x, wq_hi, wq_lo):
    n = x.shape[0]
    nq = wq_hi.shape[1]
    return pl.pallas_call(
        _peer_query_kernel,
        grid=(n // ROW_TILE,),
        in_specs=[pl.BlockSpec((ROW_TILE, D_MODEL), lambda i: (i, 0)),
                  pl.BlockSpec((D_MODEL, nq), lambda i: (0, 0)),
                  pl.BlockSpec((D_MODEL, nq), lambda i: (0, 0))],
        out_specs=pl.BlockSpec((ROW_TILE, nq), lambda i: (i, 0)),
        out_shape=jax.ShapeDtypeStruct((n, nq), f32),
        compiler_params=_cparams("parallel"),
        name="peer_query",
    )(x, wq_hi, wq_lo)


def _transpose_cast_kernel(x_ref, o_ref):
    o_ref[...] = x_ref[...].T.astype(bf16)


def _transpose_bf16(x):
    e, d = x.shape
    return pl.pallas_call(
        _transpose_cast_kernel,
        grid=(e // PEER_TE,),
        in_specs=[pl.BlockSpec((PEER_TE, d), lambda i: (i, 0))],
        out_specs=pl.BlockSpec((d, PEER_TE), lambda i: (0, i)),
        out_shape=jax.ShapeDtypeStruct((d, e), bf16),
        compiler_params=_cparams("parallel"),
        name="transpose_bf16",
    )(x)


def _top_values(s, n, want_rank=False):
    vals = []
    rank = jnp.full(s.shape, float(n), f32) if want_rank else None
    for j in range(n):
        m = jnp.max(s, axis=0, keepdims=True)
        vals.append(m)
        hit = s == m
        if want_rank:
            rank = jnp.where(hit, float(j), rank)
        s = jnp.where(hit, -jnp.inf, s)
    vals = jnp.concatenate(vals, axis=0)
    return (vals, rank) if want_rank else vals


def _peer_router(s1, s2):
    v1 = _top_values(s1, PEER_TOPK)
    v2, rank2 = _top_values(s2, PEER_TOPK, want_rank=True)
    n_take = PEER_TOPK + 1
    cands = [v1[0:1] + v2]
    sub = lax.broadcasted_iota(jnp.int32, (8, 1), 0)
    for a in range(1, PEER_TOPK):
        lim = n_take // (a + 1)
        assert lim <= 8
        ca = v1[a:a + 1] + v2[0:8]
        cands.append(ca if lim == 8 else jnp.where(sub < lim, ca, -jnp.inf))
    cand = jnp.concatenate(cands, axis=0)
    top = _top_values(cand, n_take)
    tau = 0.5 * (top[PEER_TOPK - 1:PEER_TOPK] + top[PEER_TOPK:PEER_TOPK + 1])
    z = jnp.sum(jnp.where(cand > tau, jnp.exp(cand - top[0:1]), 0.0), axis=0, keepdims=True)
    last = PEER_TOPK - 1
    e1 = jnp.where(s1 >= v1[last:last + 1], jnp.exp(s1 - v1[0:1]) / z, 0.0)
    p = jnp.where(s2 >= v2[last:last + 1], jnp.exp(s2 - v2[0:1]), 0.0)
    kappa = jnp.zeros(s1.shape, f32)
    for a in range(PEER_TOPK):
        count = jnp.sum(jnp.where(v1[a:a + 1] + v2 > tau, 1.0, 0.0), axis=0, keepdims=True)
        kappa = jnp.where(s1 == v1[a:a + 1], count, kappa)
    return e1, kappa, p.astype(bf16), rank2.astype(bf16)


INV_SQRT2 = 1.0 / math.sqrt(2.0)


def _peer_kernel(alpha, n_chunks, x_ref, q_ref, keys_ref, u_ref, vt_ref, ln_ref, o_ref,
                 xt_scr, e1_scr, kap_scr, p_scr, rank_scr, acc_scr, g_scr, ht_scr, w_scr):
    j = pl.program_id(1)
    tn = x_ref.shape[0]

    @pl.when(j == 0)
    def _():
        xt_scr[...] = x_ref[...].T.astype(bf16)
        acc_scr[...] = jnp.zeros(acc_scr.shape, f32)
        g_scr[...] = jnp.zeros(g_scr.shape, bf16)
        ht_scr[...] = jnp.zeros(ht_scr.shape, f32)
        w_scr[...] = jnp.zeros(w_scr.shape, bf16)

        def route(h, carry):
            col = pl.multiple_of(h * 2 * PEER_HALF, 2 * PEER_HALF)
            q1 = q_ref[:, pl.ds(col, PEER_HALF)]
            q2 = q_ref[:, pl.ds(col + PEER_HALF, PEER_HALF)]
            s1 = _dot3(keys_ref[2 * h], q1, NT)
            s2 = _dot3(keys_ref[2 * h + 1], q2, NT)
            e1, kappa, p, rank2 = _peer_router(s1, s2)
            e1_scr[h] = e1
            kap_scr[h] = kappa
            p_scr[h] = p
            rank_scr[h] = rank2
            return carry

        lax.fori_loop(0, PEER_HEADS, route, 0)

    cur, prev = j % 2, (j + 1) % 2
    acc_scr[...] += _dot(vt_ref[...], g_scr[cur])
    hh = ht_scr[prev]
    g_scr[prev] = w_scr[prev] * (0.5 * hh * (1.0 + lax.erf(hh * INV_SQRT2))).astype(bf16)
    ht_scr[cur] = _dot(u_ref[...], xt_scr[...])
    jc = jnp.minimum(j, n_chunks - 1)
    n_rows = PEER_TE // PEER_NKEYS
    zero = jnp.zeros((), bf16)
    for rr in range(n_rows):
        r = jc * n_rows + rr
        w = jnp.zeros((PEER_NKEYS, tn), bf16)
        for h in range(PEER_HEADS):
            sel = rank_scr[h] < kap_scr[h, pl.ds(r, 1), :].astype(bf16)
            w = w + jnp.where(sel, p_scr[h], zero) * e1_scr[h, pl.ds(r, 1), :].astype(bf16)
        w_scr[cur, rr * PEER_NKEYS:(rr + 1) * PEER_NKEYS, :] = w

    @pl.when(j == n_chunks + 1)
    def _():
        y = acc_scr[...].T
        o_ref[...] = _layer_norm(alpha * x_ref[...] + y, ln_ref[0:1, :], ln_ref[1:2, :])


def _peer_ffn(x, wq_hi, wq_lo, keys, u_bf, vt_bf, ln, alpha):
    n = x.shape[0]
    n_exp = u_bf.shape[0]
    assert n % PEER_TN == 0 and n_exp == PEER_NKEYS * PEER_NKEYS and n_exp % PEER_TE == 0
    n_chunks = n_exp // PEER_TE
    q = _peer_query(x, wq_hi, wq_lo)
    nq = q.shape[1]
    head_scr = pltpu.VMEM((PEER_HEADS, PEER_NKEYS, PEER_TN), f32)
    head_scr16 = pltpu.VMEM((PEER_HEADS, PEER_NKEYS, PEER_TN), bf16)
    return pl.pallas_call(
        functools.partial(_peer_kernel, alpha, n_chunks),
        grid=(n // PEER_TN, n_chunks + 2),
        in_specs=[
            pl.BlockSpec((PEER_TN, D_MODEL), lambda i, j: (i, 0)),
            pl.BlockSpec((PEER_TN, nq), lambda i, j: (i, 0)),
            pl.BlockSpec((2 * PEER_HEADS, PEER_NKEYS, PEER_HALF), lambda i, j: (0, 0, 0)),
            pl.BlockSpec((PEER_TE, D_MODEL), lambda i, j: (jnp.minimum(j, n_chunks - 1), 0)),
            pl.BlockSpec((D_MODEL, PEER_TE), lambda i, j: (0, jnp.clip(j - 2, 0, n_chunks - 1))),
            pl.BlockSpec((2, D_MODEL), lambda i, j: (0, 0)),
        ],
        out_specs=pl.BlockSpec((PEER_TN, D_MODEL), lambda i, j: (i, 0)),
        out_shape=jax.ShapeDtypeStruct((n, D_MODEL), f32),
        scratch_shapes=[pltpu.VMEM((D_MODEL, PEER_TN), bf16), head_scr, head_scr, head_scr16, head_scr16,
                        pltpu.VMEM((D_MODEL, PEER_TN), f32), pltpu.VMEM((2, PEER_TE, PEER_TN), bf16),
                        pltpu.VMEM((2, PEER_TE, PEER_TN), f32), pltpu.VMEM((2, PEER_TE, PEER_TN), bf16)],
        compiler_params=_cparams("parallel", "arbitrary"),
        name="peer_ffn",
    )(x, q, keys, u_bf, vt_bf, ln)


POOL_TM = 256


def kernel(x_prompt, x_sample, cache_k, cache_v, cache_logf, page_table, state_wkv, state_shift, state_pool,
           meta, w_in, b_f, w_o, mu_shift, w0, w2, a0, a2, g2, k_k, k_a, r_k, gn_g, gn_b, w_pool, pool_scale,
           ln_g, ln_b, peer_wq, peer_keys, peer_u, peer_v):
    prm = dict(mu_shift=mu_shift, w0=w0, w2=w2, a0=a0, a2=a2, g2=g2, k_k=k_k, k_a=k_a, r_k=r_k,
               gn_g=gn_g, gn_b=gn_b)
    bp, seq, d = x_prompt.shape
    db, dec_seq, _ = x_sample.shape
    assert bp == 1 and d == D_MODEL
    depth = ln_g.shape[0]
    alpha = (2 * depth) ** 0.25
    n_sample = db * dec_seq
    t_p = seq + N_META
    n_valid = n_sample + t_p
    n_rows = n_sample + -(-t_p // FOX_TQ) * FOX_TQ
    n_rows = -(-n_rows // ROW_TILE) * ROW_TILE
    h = jnp.concatenate([x_sample.reshape(n_sample, d), meta.astype(f32), x_prompt[0],
                         jnp.zeros((n_rows - n_valid, d), f32)], axis=0)

    ks, vs, lfs, wkvs, shifts, pools = ([[], []] for _ in range(6))
    n_qkv = A_IN + 3 * D_B
    for i in range(depth):
        j = i // 2
        ln1 = jnp.stack([ln_g[i, 0], ln_b[i, 0]])
        ln2 = jnp.stack([ln_g[i, 1], ln_b[i, 1]])
        if i % 2 == 0:
            w_cat = jnp.concatenate([w_in[j][:, :n_qkv],
                                     jnp.pad(w_in[j][:, n_qkv:], ((0, 0), (0, LANES - H_B)))], axis=1).astype(bf16)
            b_pad = jnp.pad(b_f[j], (0, LANES - H_B)).reshape(1, LANES)
            pa, kv, qkv16, lf = _in_projection(h, w_cat, b_pad)
            ya, wkv_p, wkv_s = _rwkv_mixer(pa, state_shift[j], state_wkv[j], prm, j, n_sample, n_valid, dec_seq)
            yb = _fox_attention(qkv16, kv, lf, cache_k[j], cache_v[j], cache_logf[j], page_table,
                                n_sample, n_valid, dec_seq)
            h1 = _out_projection(ya, yb, h, w_o[j].astype(bf16), ln1, alpha)
            for grp, lo, hi, lead in ((0, n_sample, n_valid, (bp, t_p)), (1, 0, n_sample, (db, dec_seq))):
                ks[grp].append(kv[lo:hi, :D_B].reshape(lead + (H_B, HEAD_DIM)))
                vs[grp].append(kv[lo:hi, D_B:].reshape(lead + (H_B, HEAD_DIM)))
                lfs[grp].append(lf[lo:hi, :H_B].reshape(lead + (H_B,)))
                shifts[grp].append(pa[lo:hi].reshape(lead + (A_IN,))[:, -1])
            wkvs[0].append(wkv_p)
            wkvs[1].append(wkv_s)
        else:
            ext_s = jnp.concatenate([jnp.zeros((db, POOL_HIST - POOL_BUF, d), f32), state_pool[j],
                                     h[:n_sample].reshape(db, dec_seq, d)], axis=1)
            pools[0].append(h[n_valid - POOL_BUF:n_valid][None])
            pools[1].append(ext_s[:, -POOL_BUF:])
            w_bf = w_pool[j].astype(bf16)
            scale = pool_scale[j].reshape(1, d)
            first = n_sample // POOL_TM
            hp = _pool_mixer(h, w_bf, scale, ln1, alpha, POOL_TM, first, n_rows // POOL_TM - first, True, n_sample)
            ext_len = ext_s.shape[1]
            hs = _pool_mixer(ext_s.reshape(db * ext_len, d), w_bf, scale, ln1, alpha, db * ext_len, 0, 1, False, 0)
            hs = hs.reshape(db, ext_len, d)[:, POOL_HIST:].reshape(n_sample, d)
            h1 = jnp.concatenate([hs, hp[n_sample:]], axis=0)
        wq = peer_wq[i]
        wq_hi = wq.astype(bf16)
        wq_lo = (wq - wq_hi.astype(f32)).astype(bf16)
        h = _peer_ffn(h1, wq_hi, wq_lo, peer_keys[i].reshape(2 * PEER_HEADS, PEER_NKEYS, PEER_HALF),
                      peer_u[i].astype(bf16), _transpose_bf16(peer_v[i]), ln2, alpha)

    y_prompt = h[n_sample + N_META:n_valid][None]
    y_sample = h[:n_sample].reshape(db, dec_seq, d)
    st = lambda xs: jnp.stack(xs)
    return (y_prompt, y_sample,
            st(ks[0]), st(vs[0]), st(lfs[0]), st(wkvs[0]), st(shifts[0]), st(pools[0]),
            st(ks[1]), st(vs[1]), st(lfs[1]), st(wkvs[1]), st(shifts[1]), st(pools[1]))
```

```python
import functools
import math

import jax
import jax.numpy as jnp
from jax import lax
from jax.experimental import pallas as pl
from jax.experimental.pallas import tpu as pltpu

f32 = jnp.float32
bf16 = jnp.bfloat16

D_MODEL = 1024
N_META = 16
HEAD_DIM = 64
D_A = D_MODEL // 2
D_B = D_MODEL - D_A
H_A = D_A // HEAD_DIM
H_B = D_B // HEAD_DIM
LORA_W = 64
LORA_A = 64
LORA_G = 128
A_K = D_A
A_V = 2 * D_A
A_W = 3 * D_A
A_IN = A_W + LORA_W + LORA_A + LORA_G
GN_EPS = HEAD_DIM * 1e-5
POOL_WINDOWS = (2, 4, 8, 16)
POOL_G = D_MODEL // len(POOL_WINDOWS)
POOL_BUF = max(POOL_WINDOWS) - 1
PEER_HEADS = 8
PEER_NKEYS = 128
PEER_TOPK = 16
PEER_HALF = 128
LN_EPS = 1e-5
NEG = -1e30

LANES = 128
PAIR = 2 * HEAD_DIM
N_PAIR = D_A // PAIR
ROW_TILE = 512
SCAN_C = 64
INV_BLOCK = 16
FOX_TQ = 256
FOX_TK = 256
PAGES_PER_STEP = 8
PEER_TN = 256
PEER_TE = 2048
VMEM_LIMIT = 56 * 1024 * 1024


def _cparams(*sem):
    return pltpu.CompilerParams(dimension_semantics=sem, vmem_limit_bytes=VMEM_LIMIT)


def _dot(a, b, dims=(((1,), (0,)), ((), ()))):
    return lax.dot_general(a, b, dims, preferred_element_type=f32)


NT = (((1,), (1,)), ((), ()))
TN = (((0,), (0,)), ((), ()))
NN = (((1,), (0,)), ((), ()))


def _split(x):
    hi = x.astype(bf16)
    lo = (x - hi.astype(f32)).astype(bf16)
    return hi, lo


def _dot3(a, b, dims=NN):
    ah, al = _split(a)
    bh, bl = _split(b)
    return _dot(ah, bh, dims) + (_dot(al, bh, dims) + _dot(ah, bl, dims))


def _dot_exact_rhs(a, b_bf, dims=NN):
    a0 = a.astype(bf16)
    r1 = a - a0.astype(f32)
    a1 = r1.astype(bf16)
    a2 = (r1 - a1.astype(f32)).astype(bf16)
    return _dot(a0, b_bf, dims) + (_dot(a1, b_bf, dims) + _dot(a2, b_bf, dims))


def _dot_exact_lhs(a_bf, b, dims=NN):
    b0 = b.astype(bf16)
    r1 = b - b0.astype(f32)
    b1 = r1.astype(bf16)
    b2 = (r1 - b1.astype(f32)).astype(bf16)
    return _dot(a_bf, b0, dims) + (_dot(a_bf, b1, dims) + _dot(a_bf, b2, dims))


def _layer_norm(x, g, b):
    mu = jnp.mean(x, axis=-1, keepdims=True)
    xc = x - mu
    var = jnp.mean(xc * xc, axis=-1, keepdims=True)
    return xc * lax.rsqrt(var + LN_EPS) * g + b


def _head_ones():
    r = lax.broadcasted_iota(jnp.int32, (D_A, D_A), 0) // HEAD_DIM
    c = lax.broadcasted_iota(jnp.int32, (D_A, D_A), 1) // HEAD_DIM
    return jnp.where(r == c, 1.0, 0.0).astype(bf16)


def _proj_kernel(x_ref, w_ref, bf_ref, pa_ref, kv_ref, qkv16_ref, lf_ref):
    x = x_ref[...].astype(bf16)
    y = _dot(x, w_ref[...])
    pa_ref[...] = y[:, :A_IN]
    q = y[:, A_IN:A_IN + D_B]
    kv = y[:, A_IN + D_B:A_IN + 3 * D_B]
    kv_ref[...] = kv
    qkv16_ref[:, :D_B] = (q * (HEAD_DIM ** -0.5)).astype(bf16)
    qkv16_ref[:, D_B:] = kv.astype(bf16)
    lf_ref[...] = jax.nn.log_sigmoid(y[:, A_IN + 3 * D_B:] + bf_ref[...])


def _in_projection(x, w_bf, b_f_pad):
    n = x.shape[0]
    nw = w_bf.shape[1]
    return pl.pallas_call(
        _proj_kernel,
        grid=(n // ROW_TILE,),
        in_specs=[
            pl.BlockSpec((ROW_TILE, D_MODEL), lambda i: (i, 0)),
            pl.BlockSpec((D_MODEL, nw), lambda i: (0, 0)),
            pl.BlockSpec((1, LANES), lambda i: (0, 0)),
        ],
        out_specs=[
            pl.BlockSpec((ROW_TILE, A_IN), lambda i: (i, 0)),
            pl.BlockSpec((ROW_TILE, 2 * D_B), lambda i: (i, 0)),
            pl.BlockSpec((ROW_TILE, 3 * D_B), lambda i: (i, 0)),
            pl.BlockSpec((ROW_TILE, LANES), lambda i: (i, 0)),
        ],
        out_shape=[
            jax.ShapeDtypeStruct((n, A_IN), f32),
            jax.ShapeDtypeStruct((n, 2 * D_B), f32),
            jax.ShapeDtypeStruct((n, 3 * D_B), bf16),
            jax.ShapeDtypeStruct((n, LANES), f32),
        ],
        compiler_params=_cparams("parallel"),
        name="in_projection",
    )(x, w_bf, b_f_pad)


PREP_TM = 256


def _prep_kernel(n_sample, n_valid, dec_seq,
                 pa_ref, prev_ref, ss_ref, mu_ref, vec_ref, w2_ref, a2_ref, g2_ref,
                 r_o, k_o, v_o, kk_o, b_o, lw_o, g_o, bonus_o):
    i = pl.program_id(0)
    tm = pa_ref.shape[0]
    pa = pa_ref[...]
    local = lax.broadcasted_iota(jnp.int32, (tm, 1), 0)
    row = i * tm + local
    prev = pltpu.roll(pa, 1, 0)
    prev = jnp.where(local == 0, prev_ref[7:8, :], prev)
    prev = jnp.where(row == n_sample, 0.0, prev)
    seq_start = jnp.logical_and(row < n_sample, row % dec_seq == 0)
    prev = jnp.where(seq_start, ss_ref[...], prev)
    xs = pa + (prev - pa) * mu_ref[...]

    w0 = vec_ref[0:1, :]
    a0 = vec_ref[1:2, :]
    k_k = vec_ref[2:3, :]
    k_a = vec_ref[3:4, :]
    r_k = vec_ref[4:5, :]
    r = xs[:, :A_K]
    k = xs[:, A_K:A_V]
    v = xs[:, A_V:A_W]
    wa = xs[:, A_W:A_W + LORA_W + LORA_A]
    xg = xs[:, A_W + LORA_W + LORA_A:]
    w = -jax.nn.softplus(-(w0 + _dot3(jnp.tanh(wa), w2_ref[...]))) - 0.5
    lw = -jnp.exp(w)
    a = jax.nn.sigmoid(a0 + _dot3(wa, a2_ref[...]))
    g = _dot(jax.nn.sigmoid(xg).astype(bf16), g2_ref[...].astype(bf16))
    ones = _head_ones()
    kk = k * k_k
    kk = kk / jnp.maximum(jnp.sqrt(_dot_exact_rhs(kk * kk, ones)), 1e-12)
    k = k * (1.0 + (a - 1.0) * k_a)
    bonus = _dot_exact_rhs(r * k * r_k, ones) * v
    valid = row < n_valid
    r_o[...] = r
    k_o[...] = jnp.where(valid, k, 0.0)
    v_o[...] = jnp.where(valid, v, 0.0)
    kk_o[...] = kk
    b_o[...] = jnp.where(valid, kk * a, 0.0)
    lw_o[...] = jnp.where(valid, lw, 0.0)
    g_o[...] = g
    bonus_o[...] = bonus


def _rwkv_prep(pa, ss_rows, mu, vecs, w2p, a2p, g2, n_sample, n_valid, dec_seq):
    n = pa.shape[0]
    assert n_sample == PREP_TM and n % PREP_TM == 0
    row_spec = pl.BlockSpec((PREP_TM, D_A), lambda i: (i, 0))
    const = lambda shape: pl.BlockSpec(shape, lambda i: (0, 0))
    return pl.pallas_call(
        functools.partial(_prep_kernel, n_sample, n_valid, dec_seq),
        grid=(n // PREP_TM,),
        in_specs=[
            pl.BlockSpec((PREP_TM, A_IN), lambda i: (i, 0)),
            pl.BlockSpec((8, A_IN), lambda i: (jnp.maximum(i * (PREP_TM // 8) - 1, 0), 0)),
            const((PREP_TM, A_IN)),
            const((1, A_IN)),
            const((8, D_A)),
            const((LORA_W + LORA_A, D_A)),
            const((LORA_W + LORA_A, D_A)),
            const((LORA_G, D_A)),
        ],
        out_specs=[row_spec] * 8,
        out_shape=[jax.ShapeDtypeStruct((n, D_A), f32)] * 8,
        compiler_params=_cparams("parallel"),
        name="rwkv_prep",
    )(pa, pa, ss_rows, mu, vecs, w2p, a2p, g2)


def _stack_heads(x, lane_a):
    return jnp.concatenate([jnp.where(lane_a, x, 0.0), jnp.where(lane_a, 0.0, x)], axis=0)


def _unit_lower_inverse(x):
    n = x.shape[0]
    ri = lax.broadcasted_iota(jnp.int32, (n, n), 0)
    ci = lax.broadcasted_iota(jnp.int32, (n, n), 1)
    eye = jnp.where(ri == ci, 1.0, 0.0)
    d = jnp.where(ri // INV_BLOCK == ci // INV_BLOCK, x, 0.0)
    off = x - d
    d2 = _dot3(d, d)
    yield
    d4 = _dot3(d2, d2)
    yield
    d8 = _dot3(d4, d4)
    p = eye - d
    p = p + _dot3(p, d2)
    yield
    p = p + _dot3(p, d4)
    yield
    dinv = p + _dot3(p, d8)
    yield
    m = _dot3(dinv, off)
    yield
    m2 = _dot3(m, m)
    yield
    q = eye - m
    q = q + _dot3(q, m2)
    yield
    return _dot3(q, dinv)


def _scan_pair(r, k, v, kk, b, lw, g, bonus, gn, S):
    C = r.shape[0]
    assert C == SCAN_C and SCAN_C // INV_BLOCK == 4 and 2 * C == PAIR
    lane = lax.broadcasted_iota(jnp.int32, (1, PAIR), 1)
    lane_a = lane < HEAD_DIM
    t_idx = lax.broadcasted_iota(jnp.int32, (C, 1), 0)
    tri = jnp.where(lax.broadcasted_iota(jnp.int32, (C, C), 1) <= t_idx, 1.0, 0.0).astype(bf16)
    cs = _dot_exact_lhs(tri, lw)
    total = cs[C - 1:C, :]
    kp = kk * jnp.exp(cs - lw)
    rp = r * jnp.exp(cs)
    g_inv = jnp.exp(-cs)
    g_tail = jnp.exp(total - cs)
    kt, bt = k * g_inv, b * g_inv
    kh, bh = k * g_tail, b * g_tail

    st = lambda x: _stack_heads(x, lane_a)
    sc = _dot3(jnp.concatenate([kp, rp], axis=0),
               jnp.concatenate([st(bt), st(kt)], axis=0), NT)
    yield
    s_idx = lane % C
    strict = s_idx < t_idx
    incl = s_idx <= t_idx
    lb = jnp.where(strict, sc[:C, :PAIR], 0.0)
    lk = jnp.where(strict, sc[:C, PAIR:], 0.0)
    pb = jnp.where(incl, sc[C:, :PAIR], 0.0)
    pk = jnp.where(incl, sc[C:, PAIR:], 0.0)

    lkv = _dot3(lk, st(v))
    yield
    tinv = yield from _unit_lower_inverse(st(lb))
    yield
    tcat = tinv[:C] + tinv[C:]
    w12 = _dot3(tcat, jnp.concatenate([st(lkv), st(kp)], axis=1))
    yield
    w1, w2 = w12[:, :PAIR], w12[:, PAIR:]
    pbw = _dot(pb.astype(bf16), jnp.concatenate([st(w1), st(w2)], axis=1).astype(bf16))
    pkv = _dot(pk.astype(bf16), st(v).astype(bf16))
    yield
    q2 = rp - pbw[:, PAIR:]
    y = _dot(q2.astype(bf16), S.astype(bf16), NT) + (pkv - pbw[:, :PAIR])

    ri = lax.broadcasted_iota(jnp.int32, (PAIR, PAIR), 0)
    ci = lax.broadcasted_iota(jnp.int32, (PAIR, PAIR), 1)
    same_head = (ri // HEAD_DIM) == (ci // HEAD_DIM)
    wb = _dot3(jnp.concatenate([w2, w1], axis=1), bh, TN)
    yield
    g2m = jnp.where(ri == ci, jnp.exp(total), 0.0) - jnp.where(same_head, wb[:PAIR], 0.0)
    g1m = jnp.where(same_head, _dot3(v, kh, TN) - wb[PAIR:], 0.0)
    yield
    s_new = _dot3(S, g2m) + g1m
    yield

    ones = jnp.where(same_head, 1.0, 0.0).astype(bf16)
    mean = _dot_exact_rhs(y, ones) * (1.0 / HEAD_DIM)
    yc = y - mean
    var = _dot_exact_rhs(yc * yc, ones) * (1.0 / HEAD_DIM)
    yn = yc * lax.rsqrt(var + GN_EPS) * gn[0:1, :] + gn[1:2, :]
    return (yn + bonus) * g, s_new


def _scan_kernel(per_chunk_state, *refs):
    if per_chunk_state:
        (r_ref, k_ref, v_ref, kk_ref, b_ref, lw_ref, g_ref, bonus_ref, gn_ref, s0_ref,
         ya_ref, sout_ref, s_scr) = refs
        s_scr[...] = s0_ref[0]
    else:
        (r_ref, k_ref, v_ref, kk_ref, b_ref, lw_ref, g_ref, bonus_ref, gn_ref,
         ya_ref, sout_ref, s_scr) = refs

        @pl.when(pl.program_id(0) == 0)
        def _():
            s_scr[...] = jnp.zeros(s_scr.shape, f32)

    in_refs = (r_ref, k_ref, v_ref, kk_ref, b_ref, lw_ref, g_ref, bonus_ref, gn_ref)
    lanes = [slice(p * PAIR, (p + 1) * PAIR) for p in range(N_PAIR)]
    chains = [_scan_pair(*(x[:, sl] for x in in_refs), s_scr[p]) for p, sl in enumerate(lanes)]
    pending = list(range(N_PAIR))
    while pending:
        for p in list(pending):
            try:
                next(chains[p])
            except StopIteration as done:
                ya, s_new = done.value
                ya_ref[:, lanes[p]] = ya
                s_scr[p] = s_new
                sout_ref[0, p] = s_new
                pending.remove(p)


def _rwkv_scan(arrs, gn, n_rows_out, row_block_off, n_chunks, s0=None):
    per_chunk = s0 is not None
    blk = pl.BlockSpec((SCAN_C, D_A), lambda c: (c + row_block_off, 0))
    in_specs = [blk] * 8 + [pl.BlockSpec((2, D_A), lambda c: (0, 0))]
    args = list(arrs) + [gn]
    if per_chunk:
        in_specs.append(pl.BlockSpec((1, N_PAIR, PAIR, PAIR), lambda c: (c, 0, 0, 0)))
        args.append(s0)
        n_states = n_chunks
        s_map = lambda c: (c, 0, 0, 0)
    else:
        n_states = 1
        s_map = lambda c: (0, 0, 0, 0)
    return pl.pallas_call(
        functools.partial(_scan_kernel, per_chunk),
        grid=(n_chunks,),
        in_specs=in_specs,
        out_specs=[blk, pl.BlockSpec((1, N_PAIR, PAIR, PAIR), s_map)],
        out_shape=[jax.ShapeDtypeStruct((n_rows_out, D_A), f32),
                   jax.ShapeDtypeStruct((n_states, N_PAIR, PAIR, PAIR), f32)],
        scratch_shapes=[pltpu.VMEM((N_PAIR, PAIR, PAIR), f32)],
        compiler_params=_cparams("arbitrary"),
        name="rwkv_scan",
    )(*args)


def _pair_states(state):
    b = state.shape[0]
    s = state.reshape(b, N_PAIR, 2, HEAD_DIM, HEAD_DIM)
    out = jnp.zeros((b, N_PAIR, 2, HEAD_DIM, 2, HEAD_DIM), f32)
    out = out.at[:, :, 0, :, 0, :].set(s[:, :, 0]).at[:, :, 1, :, 1, :].set(s[:, :, 1])
    return out.reshape(b, N_PAIR, PAIR, PAIR)


def _unpair_states(s):
    b = s.shape[0]
    s = s.reshape(b, N_PAIR, 2, HEAD_DIM, 2, HEAD_DIM)
    return jnp.stack([s[:, :, 0, :, 0, :], s[:, :, 1, :, 1, :]], axis=2).reshape(b, H_A, HEAD_DIM, HEAD_DIM)


def _rwkv_mixer(pa, state_shift, state_wkv, prm, j, n_sample, n_valid, dec_seq):
    n = pa.shape[0]
    db = n_sample // dec_seq
    ss_rows = jnp.zeros((db, dec_seq, A_IN), f32).at[:, 0].set(state_shift).reshape(n_sample, A_IN)
    vecs = jnp.zeros((8, D_A), f32)
    for idx, name in enumerate(("w0", "a0", "k_k", "k_a", "r_k")):
        vecs = vecs.at[idx].set(prm[name][j].reshape(D_A))
    zpad = jnp.zeros((LORA_W, D_A), f32)
    w2p = jnp.concatenate([prm["w2"][j], zpad], axis=0)
    a2p = jnp.concatenate([zpad, prm["a2"][j]], axis=0)
    arrs = _rwkv_prep(pa, ss_rows, prm["mu_shift"][j].reshape(1, A_IN), vecs, w2p, a2p, prm["g2"][j],
                      n_sample, n_valid, dec_seq)
    gn = jnp.stack([prm["gn_g"][j], prm["gn_b"][j]])
    n_chunks = -(-(n_valid - n_sample) // SCAN_C)
    assert n_sample % SCAN_C == 0 and n_sample + n_chunks * SCAN_C <= n and dec_seq <= SCAN_C
    ya_p, s_p = _rwkv_scan(arrs, gn, n, n_sample // SCAN_C, n_chunks)

    def pad_seq(x):
        x = x[:n_sample].reshape(db, dec_seq, D_A)
        return jnp.pad(x, ((0, 0), (0, SCAN_C - dec_seq), (0, 0))).reshape(db * SCAN_C, D_A)

    ya_s, s_s = _rwkv_scan([pad_seq(x) for x in arrs], gn, db * SCAN_C, 0, db, s0=_pair_states(state_wkv))
    ya_s = ya_s.reshape(db, SCAN_C, D_A)[:, :dec_seq].reshape(n_sample, D_A)
    ya = jnp.concatenate([ya_s, ya_p[n_sample:]], axis=0)
    return ya, _unpair_states(s_p[0][None]), _unpair_states(s_s)


CUM_BLOCK = 256


def _upper_ones(n):
    r = lax.broadcasted_iota(jnp.int32, (n, n), 0)
    c = lax.broadcasted_iota(jnp.int32, (n, n), 1)
    return jnp.where(r <= c, 1.0, 0.0).astype(bf16)


def _cumsum_kernel(x_ref, d_ref, c_ref, carry):
    @pl.when(pl.program_id(0) == 0)
    def _():
        carry[...] = jnp.zeros_like(carry)

    x = x_ref[...]
    n = x.shape[1]
    d_ref[...] = _dot_exact_rhs(x, _upper_ones(n))
    c_ref[0] = carry[...]
    carry[...] = carry[...] + _dot_exact_rhs(x, jnp.ones((n, LANES), bf16))


def _logf_cumsum(lf_t, first_block):
    h, n = lf_t.shape
    nb = n // CUM_BLOCK
    return pl.pallas_call(
        _cumsum_kernel,
        grid=(nb - first_block,),
        in_specs=[pl.BlockSpec((h, CUM_BLOCK), lambda i: (0, i + first_block))],
        out_specs=[pl.BlockSpec((h, CUM_BLOCK), lambda i: (0, i + first_block)),
                   pl.BlockSpec((1, h, LANES), lambda i: (i + first_block, 0, 0))],
        out_shape=[jax.ShapeDtypeStruct((h, n), f32), jax.ShapeDtypeStruct((nb, h, LANES), f32)],
        scratch_shapes=[pltpu.VMEM((h, LANES), f32)],
        compiler_params=_cparams("arbitrary"),
        name="logf_cumsum",
    )(lf_t)


FOX_KW = 2 * PAIR


def _fox_prompt_kernel(row0, cref_ref, q_ref, k_ref, va_ref, vb_ref, o_ref, s_scr, acc_scr):
    pair = pl.program_id(0)
    i = pl.program_id(1)
    q = q_ref[...]
    lane = lax.broadcasted_iota(jnp.int32, (FOX_TQ, PAIR), 1)
    lane_a = lane < HEAD_DIM
    zero = jnp.zeros_like(q)

    def query_t(qh, col):
        bias = jnp.where(jnp.logical_or(lane == col, lane == col + 1), -1.0, 0.0)
        qa = jnp.concatenate([qh.astype(f32), bias], axis=1)
        return qa.T.astype(bf16)

    q_t = (query_t(jnp.where(lane_a, q, zero), 0), query_t(jnp.where(lane_a, zero, q), 2))
    v_refs = (va_ref, vb_ref)
    key_idx = lax.broadcasted_iota(jnp.int32, (FOX_TK, 1), 0)
    qry_idx = lax.broadcasted_iota(jnp.int32, (1, FOX_TQ), 1)
    blk0 = row0 // FOX_TK

    def block_start(j):
        return pl.multiple_of(row0 + j * FOX_TK, FOX_TK)

    def scores(j):
        kb = k_ref[pl.ds(block_start(j), FOX_TK), :]
        return tuple(_dot(kb, q_t[hh]) for hh in range(2))

    def block_max(j, s_pair):
        return tuple(jnp.max(s_pair[hh], axis=0, keepdims=True) - cref_ref[(blk0 + j) * H_B + 2 * pair + hh]
                     for hh in range(2))

    def update(j, s_pair, mt_pair, ms):
        ps, ms_new, alphas = [], [], []
        for hh in range(2):
            cref = cref_ref[(blk0 + j) * H_B + 2 * pair + hh]
            m_new = jnp.maximum(ms[hh], mt_pair[hh])
            ps.append(jnp.exp((s_pair[hh] - (m_new + cref)).astype(bf16)))
            ms_new.append(m_new)
            alphas.append(jnp.exp(ms[hh] - m_new))
        start = block_start(j)
        for hh in range(2):
            acc_scr[hh] = acc_scr[hh] * alphas[hh] + _dot(v_refs[hh][pl.ds(start, FOX_TK), :], ps[hh], TN)
        return tuple(ms_new)

    def step(j, state):
        ms, mt_cur = state
        s_cur = (s_scr[0], s_scr[1])
        s_next = scores(j + 1)
        ms = update(j, s_cur, mt_cur, ms)
        s_scr[0], s_scr[1] = s_next
        return ms, block_max(j + 1, s_next)

    neg = jnp.full((1, FOX_TQ), NEG, f32)
    acc_scr[...] = jnp.zeros(acc_scr.shape, f32)
    s0 = scores(0)
    s_scr[0], s_scr[1] = s0
    ms, _ = lax.fori_loop(0, i, step, ((neg, neg), block_max(0, s0)))
    s_last = tuple(jnp.where(key_idx <= qry_idx, s_scr[hh], NEG) for hh in range(2))
    update(i, s_last, block_max(i, s_last), ms)
    acc_a, acc_b = acc_scr[0], acc_scr[1]
    out_t = jnp.concatenate([acc_a[:HEAD_DIM] / acc_a[HEAD_DIM:HEAD_DIM + 1],
                             acc_b[HEAD_DIM:] / acc_b[0:1]], axis=0)
    o_ref[...] = out_t.T


def _fox_prompt(q16, k_aug, v_a, v_b, cref, n_sample, n_valid):
    n = q16.shape[0]
    assert FOX_TQ == FOX_TK == CUM_BLOCK and n_sample % FOX_TQ == 0
    nq = -(-(n_valid - n_sample) // FOX_TQ)
    assert n_sample + nq * FOX_TQ <= n
    qb0 = n_sample // FOX_TQ
    nb = D_B // PAIR
    return pl.pallas_call(
        functools.partial(_fox_prompt_kernel, n_sample),
        grid=(nb, nq),
        in_specs=[
            pl.BlockSpec(memory_space=pltpu.SMEM),
            pl.BlockSpec((FOX_TQ, PAIR), lambda p, i: (i + qb0, p)),
            pl.BlockSpec((n, FOX_KW), lambda p, i: (0, p)),
            pl.BlockSpec((n, PAIR), lambda p, i: (0, p)),
            pl.BlockSpec((n, PAIR), lambda p, i: (0, p)),
        ],
        out_specs=pl.BlockSpec((FOX_TQ, PAIR), lambda p, i: (i + qb0, p)),
        out_shape=jax.ShapeDtypeStruct((n, D_B), f32),
        scratch_shapes=[pltpu.VMEM((2, FOX_TK, FOX_TQ), f32), pltpu.VMEM((2, PAIR, FOX_TQ), f32)],
        compiler_params=_cparams("parallel", "arbitrary"),
        name="fox_prompt",
    )(cref, q16, k_aug, v_a, v_b)


def _fox_sample_kernel(n_steps, page_size, pt_ref, q_ref, kn_ref, vn_ref, lfn_ref, *refs):
    np_ = PAGES_PER_STEP
    k_refs, v_refs, lf_refs = refs[:np_], refs[np_:2 * np_], refs[2 * np_:3 * np_]
    o_ref, m_scr, l_scr, acc_scr, carry_scr = refs[3 * np_:]
    j = pl.program_id(1)
    ds = q_ref.shape[1]
    nrow = H_B * ds

    @pl.when(j == 0)
    def _():
        m_scr[...] = jnp.full(m_scr.shape, NEG, f32)
        l_scr[...] = jnp.zeros(l_scr.shape, f32)
        acc_scr[...] = jnp.zeros(acc_scr.shape, f32)
        carry_scr[...] = jnp.zeros(carry_scr.shape, f32)

    row_h = lax.broadcasted_iota(jnp.int32, (nrow, D_B), 0) // ds
    lane_h = lax.broadcasted_iota(jnp.int32, (nrow, D_B), 1) // HEAD_DIM
    head_sel = row_h == lane_h
    q = q_ref[0]
    qx = jnp.where(head_sel, jnp.broadcast_to(q[None], (H_B, ds, D_B)).reshape(nrow, D_B), 0.0).astype(bf16)

    def attend(kts, vts, lfs, mask):
        nk = page_size * len(kts)
        cum = lfs[0] if len(lfs) == 1 else jnp.concatenate(lfs, axis=1)
        key_lane = lax.broadcasted_iota(jnp.int32, (1, nk), 1)
        shift = 1
        while shift < nk:
            cum = cum + jnp.where(key_lane >= shift, pltpu.roll(cum, shift, 1), 0.0)
            shift *= 2
        run = carry_scr[...]
        carry_scr[...] = run + cum[:, nk - 1:nk]
        cum = cum + run
        cat = lambda xs: (xs[0] if len(xs) == 1 else jnp.concatenate(xs, axis=1)).astype(bf16)
        s = _dot(qx, cat(kts))
        s = s - jnp.broadcast_to(cum[:, None, :], (H_B, ds, nk)).reshape(nrow, nk)
        if mask is not None:
            s = jnp.where(mask, s, NEG)
        m = m_scr[...]
        m_new = jnp.maximum(m, jnp.max(s, axis=1, keepdims=True))
        alpha = jnp.exp(m - m_new)
        p = jnp.exp(s - m_new)
        l_scr[...] = l_scr[...] * alpha + jnp.sum(p, axis=1, keepdims=True)
        acc_scr[...] = acc_scr[...] * alpha + _dot(p.astype(bf16), cat(vts), NT)
        m_scr[...] = m_new

    attend([r[0] for r in k_refs], [r[0] for r in v_refs], [r[0] for r in lf_refs], None)

    @pl.when(j == n_steps - 1)
    def _():
        t_q = lax.broadcasted_iota(jnp.int32, (nrow, page_size), 0) % ds
        t_k = lax.broadcasted_iota(jnp.int32, (nrow, page_size), 1)
        attend([kn_ref[0]], [vn_ref[0]], [lfn_ref[0]], t_k <= t_q)
        o = jnp.where(head_sel, acc_scr[...] / l_scr[...], 0.0)
        out = o[0:ds]
        for h in range(1, H_B):
            out = out + o[h * ds:(h + 1) * ds]
        o_ref[...] = out


def _fox_sample(q_s, kn_t, vn_t, lfn_t, cache_kt, cache_vt, cache_lft, page_table, dec_seq):
    db = q_s.shape[0]
    n_pages = page_table.shape[1]
    page_size = cache_kt.shape[2]
    assert n_pages % PAGES_PER_STEP == 0 and page_size == LANES
    n_steps = n_pages // PAGES_PER_STEP

    def page_map(pp):
        return lambda b, j, pt: (pt[b, j * PAGES_PER_STEP + pp], 0, 0)

    batch3 = lambda shape: pl.BlockSpec((1,) + shape, lambda b, j, pt: (b, 0, 0))
    in_specs = [batch3((dec_seq, D_B)), batch3((D_B, page_size)), batch3((D_B, page_size)),
                batch3((H_B, page_size))]
    in_specs += [pl.BlockSpec((1, D_B, page_size), page_map(pp)) for pp in range(PAGES_PER_STEP)]
    in_specs += [pl.BlockSpec((1, D_B, page_size), page_map(pp)) for pp in range(PAGES_PER_STEP)]
    in_specs += [pl.BlockSpec((1, H_B, page_size), page_map(pp)) for pp in range(PAGES_PER_STEP)]
    nrow = H_B * dec_seq
    grid_spec = pltpu.PrefetchScalarGridSpec(
        num_scalar_prefetch=1,
        grid=(db, n_steps),
        in_specs=in_specs,
        out_specs=pl.BlockSpec((dec_seq, D_B), lambda b, j, pt: (b, 0)),
        scratch_shapes=[pltpu.VMEM((nrow, 1), f32), pltpu.VMEM((nrow, 1), f32),
                        pltpu.VMEM((nrow, D_B), f32), pltpu.VMEM((H_B, 1), f32)],
    )
    return pl.pallas_call(
        functools.partial(_fox_sample_kernel, n_steps, page_size),
        grid_spec=grid_spec,
        out_shape=jax.ShapeDtypeStruct((db * dec_seq, D_B), f32),
        compiler_params=_cparams("parallel", "arbitrary"),
        name="fox_sample",
    )(page_table, q_s, kn_t, vn_t, lfn_t, *([cache_kt] * PAGES_PER_STEP), *([cache_vt] * PAGES_PER_STEP),
      *([cache_lft] * PAGES_PER_STEP))


def _fox_attention(qkv16, kv, lf, cache_k, cache_v, cache_lf, page_table, n_sample, n_valid, dec_seq):
    n = qkv16.shape[0]
    db = n_sample // dec_seq
    lf_t = lf[:, :H_B].T
    delta_t, c_blk = _logf_cumsum(lf_t, n_sample // CUM_BLOCK)
    delta = delta_t.T
    d_hi = delta.astype(bf16)
    d_lo = (delta - d_hi.astype(f32)).astype(bf16)
    k16, v16 = qkv16[:, D_B:2 * D_B], qkv16[:, 2 * D_B:]
    zpad = jnp.zeros((n, FOX_KW - PAIR - 4), bf16)
    parts = []
    for p in range(D_B // PAIR):
        parts += [k16[:, p * PAIR:(p + 1) * PAIR], d_hi[:, 2 * p:2 * p + 1], d_lo[:, 2 * p:2 * p + 1],
                  d_hi[:, 2 * p + 1:2 * p + 2], d_lo[:, 2 * p + 1:2 * p + 2], zpad]
    k_aug = jnp.concatenate(parts, axis=1)
    lane_a = (jnp.arange(D_B) % PAIR) < HEAD_DIM
    one = jnp.ones((), bf16)
    yb_p = _fox_prompt(qkv16, k_aug, jnp.where(lane_a, v16, one), jnp.where(lane_a, one, v16),
                       c_blk[:, :, 0].reshape(-1), n_sample, n_valid)
    n_pool, page_size = cache_k.shape[0], cache_k.shape[1]
    q_s = qkv16[:n_sample, :D_B].astype(f32).reshape(db, dec_seq, D_B)
    pad_keys = lambda x: jnp.pad(jnp.swapaxes(x, 1, 2), ((0, 0), (0, 0), (0, page_size - dec_seq)))
    kn_t = pad_keys(kv[:n_sample, :D_B].reshape(db, dec_seq, D_B))
    vn_t = pad_keys(kv[:n_sample, D_B:].reshape(db, dec_seq, D_B))
    lfn_t = pad_keys(lf[:n_sample, :H_B].reshape(db, dec_seq, H_B))
    to_t = lambda c: jnp.transpose(c, (0, 2, 3, 1)).reshape(n_pool, D_B, page_size)
    yb_s = _fox_sample(q_s, kn_t, vn_t, lfn_t, to_t(cache_k), to_t(cache_v), jnp.swapaxes(cache_lf, 1, 2),
                       page_table, dec_seq)
    return jnp.concatenate([yb_s, yb_p[n_sample:]], axis=0)


def _outproj_kernel(alpha, ya_ref, yb_ref, h_ref, w_ref, ln_ref, o_ref):
    y = _dot(ya_ref[...].astype(bf16), w_ref[:D_A, :]) + _dot(yb_ref[...].astype(bf16), w_ref[D_A:, :])
    o_ref[...] = _layer_norm(alpha * h_ref[...] + y, ln_ref[0:1, :], ln_ref[1:2, :])


def _out_projection(ya, yb, h, w_bf, ln, alpha):
    n = h.shape[0]
    half = pl.BlockSpec((ROW_TILE, D_A), lambda i: (i, 0))
    full = pl.BlockSpec((ROW_TILE, D_MODEL), lambda i: (i, 0))
    return pl.pallas_call(
        functools.partial(_outproj_kernel, alpha),
        grid=(n // ROW_TILE,),
        in_specs=[half, half, full,
                  pl.BlockSpec((D_MODEL, D_MODEL), lambda i: (0, 0)),
                  pl.BlockSpec((2, D_MODEL), lambda i: (0, 0))],
        out_specs=full,
        out_shape=jax.ShapeDtypeStruct((n, D_MODEL), f32),
        compiler_params=_cparams("parallel"),
        name="out_projection",
    )(ya, yb, h, w_bf, ln)


POOL_HIST = POOL_BUF + 1


def _pool_kernel(prompt_mode, zero_hist_tile, row0, alpha, x_ref, hist_ref, w_ref, sc_ref, ln_ref, o_ref):
    i = pl.program_id(0) + zero_hist_tile
    tm = x_ref.shape[0]
    x = x_ref[...]
    hist = jnp.where(i == zero_hist_tile, 0.0, hist_ref[...])
    ext = jnp.concatenate([hist, x], axis=0)
    t = i * tm + lax.broadcasted_iota(jnp.int32, (tm, 1), 0) - row0
    ys = []
    for gi, win in enumerate(POOL_WINDOWS):
        s = ext[:, gi * POOL_G:(gi + 1) * POOL_G]
        span = 1
        while span < win:
            s = s + pltpu.roll(s, span, 0)
            span *= 2
        cnt = jnp.minimum(win, t + 1).astype(f32) if prompt_mode else float(win)
        diff = s[POOL_HIST:] / cnt - x[:, gi * POOL_G:(gi + 1) * POOL_G]
        ys.append(_dot(diff.astype(bf16), w_ref[gi]))
    y = jnp.concatenate(ys, axis=1) * sc_ref[...]
    o_ref[...] = _layer_norm(alpha * x + y, ln_ref[0:1, :], ln_ref[1:2, :])


def _pool_mixer(x, w_bf, scale, ln, alpha, tm, first_tile, n_tiles, prompt_mode, row0):
    n = x.shape[0]
    assert tm % POOL_HIST == 0
    hb = tm // POOL_HIST
    return pl.pallas_call(
        functools.partial(_pool_kernel, prompt_mode, first_tile, row0, alpha),
        grid=(n_tiles,),
        in_specs=[
            pl.BlockSpec((tm, D_MODEL), lambda i: (i + first_tile, 0)),
            pl.BlockSpec((POOL_HIST, D_MODEL), lambda i: (jnp.maximum((i + first_tile) * hb - 1, 0), 0)),
            pl.BlockSpec((len(POOL_WINDOWS), POOL_G, POOL_G), lambda i: (0, 0, 0)),
            pl.BlockSpec((1, D_MODEL), lambda i: (0, 0)),
            pl.BlockSpec((2, D_MODEL), lambda i: (0, 0)),
        ],
        out_specs=pl.BlockSpec((tm, D_MODEL), lambda i: (i + first_tile, 0)),
        out_shape=jax.ShapeDtypeStruct((n, D_MODEL), f32),
        compiler_params=_cparams("parallel"),
        name="pool_mixer",
    )(x, x, w_bf, scale, ln)


def _peer_query_kernel(x_ref, wh_ref, wl_ref, o_ref):
    xh, xl = _split(x_ref[...])
    o_ref[...] = _dot(xh, wh_ref[...]) + (_dot(xl, wh_ref[...]) + _dot(xh, wl_ref[...]))


def _peer_query(x, wq_hi, wq_lo):
    n = x.shape[0]
    nq = wq_hi.shape[1]
    return pl.pallas_call(
        _peer_query_kernel,
        grid=(n // ROW_TILE,),
        in_specs=[pl.BlockSpec((ROW_TILE, D_MODEL), lambda i: (i, 0)),
                  pl.BlockSpec((D_MODEL, nq), lambda i: (0, 0)),
                  pl.BlockSpec((D_MODEL, nq), lambda i: (0, 0))],
        out_specs=pl.BlockSpec((ROW_TILE, nq), lambda i: (i, 0)),
        out_shape=jax.ShapeDtypeStruct((n, nq), f32),
        compiler_params=_cparams("parallel"),
        name="peer_query",
    )(x, wq_hi, wq_lo)


def _transpose_cast_kernel(x_ref, o_ref):
    o_ref[...] = x_ref[...].T.astype(bf16)


def _transpose_bf16(x):
    e, d = x.shape
    return pl.pallas_call(
        _transpose_cast_kernel,
        grid=(e // PEER_TE,),
        in_specs=[pl.BlockSpec((PEER_TE, d), lambda i: (i, 0))],
        out_specs=pl.BlockSpec((d, PEER_TE), lambda i: (0, i)),
        out_shape=jax.ShapeDtypeStruct((d, e), bf16),
        compiler_params=_cparams("parallel"),
        name="transpose_bf16",
    )(x)


def _top_values(s, n, want_rank=False):
    vals = []
    rank = jnp.full(s.shape, float(n), f32) if want_rank else None
    for j in range(n):
        m = jnp.max(s, axis=0, keepdims=True)
        vals.append(m)
        hit = s == m
        if want_rank:
            rank = jnp.where(hit, float(j), rank)
        s = jnp.where(hit, -jnp.inf, s)
    vals = jnp.concatenate(vals, axis=0)
    return (vals, rank) if want_rank else vals


def _peer_router(s1, s2):
    v1 = _top_values(s1, PEER_TOPK)
    v2, rank2 = _top_values(s2, PEER_TOPK, want_rank=True)
    n_take = PEER_TOPK + 1
    cands = [v1[0:1] + v2]
    sub = lax.broadcasted_iota(jnp.int32, (8, 1), 0)
    for a in range(1, PEER_TOPK):
        lim = n_take // (a + 1)
        assert lim <= 8
        ca = v1[a:a + 1] + v2[0:8]
        cands.append(ca if lim == 8 else jnp.where(sub < lim, ca, -jnp.inf))
    cand = jnp.concatenate(cands, axis=0)
    top = _top_values(cand, n_take)
    tau = 0.5 * (top[PEER_TOPK - 1:PEER_TOPK] + top[PEER_TOPK:PEER_TOPK + 1])
    z = jnp.sum(jnp.where(cand > tau, jnp.exp(cand - top[0:1]), 0.0), axis=0, keepdims=True)
    last = PEER_TOPK - 1
    e1 = jnp.where(s1 >= v1[last:last + 1], jnp.exp(s1 - v1[0:1]) / z, 0.0)
    p = jnp.where(s2 >= v2[last:last + 1], jnp.exp(s2 - v2[0:1]), 0.0)
    kappa = jnp.zeros(s1.shape, f32)
    for a in range(PEER_TOPK):
        count = jnp.sum(jnp.where(v1[a:a + 1] + v2 > tau, 1.0, 0.0), axis=0, keepdims=True)
        kappa = jnp.where(s1 == v1[a:a + 1], count, kappa)
    return e1, kappa, p.astype(bf16), rank2.astype(bf16)


INV_SQRT2 = 1.0 / math.sqrt(2.0)


def _peer_kernel(alpha, n_tiles, n_chunks, x_ref, xprev_ref, q_ref, keys_ref, u_ref, vt_ref, ln_ref, o_ref,
                 xt_scr, e1_scr, kap_scr, p_scr, rank_scr, acc_scr, g_scr, ht_scr, w_scr):
    i = pl.program_id(0)
    j = pl.program_id(1)
    tn = x_ref.shape[0]
    assert n_chunks % 2 == 0 and n_chunks >= 2

    @pl.when(jnp.logical_and(i == 0, j == 0))
    def _():
        acc_scr[...] = jnp.zeros(acc_scr.shape, f32)
        g_scr[...] = jnp.zeros(g_scr.shape, bf16)
        ht_scr[...] = jnp.zeros(ht_scr.shape, f32)
        w_scr[...] = jnp.zeros(w_scr.shape, bf16)

    @pl.when(jnp.logical_and(i < n_tiles, j == 0))
    def _():
        xt_scr[...] = x_ref[...].T.astype(bf16)
        acc_scr[i % 2] = jnp.zeros(acc_scr.shape[1:], f32)

        def route(h, carry):
            col = pl.multiple_of(h * 2 * PEER_HALF, 2 * PEER_HALF)
            q1 = q_ref[:, pl.ds(col, PEER_HALF)]
            q2 = q_ref[:, pl.ds(col + PEER_HALF, PEER_HALF)]
            s1 = _dot3(keys_ref[2 * h], q1, NT)
            s2 = _dot3(keys_ref[2 * h + 1], q2, NT)
            e1, kappa, p, rank2 = _peer_router(s1, s2)
            e1_scr[h] = e1
            kap_scr[h] = kappa
            p_scr[h] = p
            rank_scr[h] = rank2
            return carry

        lax.fori_loop(0, PEER_HEADS, route, 0)

    cur, prev = j % 2, (j + 1) % 2
    acc_slot = jnp.where(j >= 2, i, i + 1) % 2
    hh = ht_scr[prev]
    g_scr[prev] = w_scr[prev] * (0.5 * hh * (1.0 + lax.erf(hh * INV_SQRT2))).astype(bf16)
    ht_scr[cur] = _dot(u_ref[...], xt_scr[...])
    acc_scr[acc_slot] += _dot(vt_ref[...], g_scr[cur])
    n_rows = PEER_TE // PEER_NKEYS
    zero = jnp.zeros((), bf16)
    for rr in range(n_rows):
        r = j * n_rows + rr
        w = jnp.zeros((PEER_NKEYS, tn), bf16)
        for h in range(PEER_HEADS):
            sel = rank_scr[h] < kap_scr[h, pl.ds(r, 1), :].astype(bf16)
            w = w + jnp.where(sel, p_scr[h], zero) * e1_scr[h, pl.ds(r, 1), :].astype(bf16)
        w_scr[cur, rr * PEER_NKEYS:(rr + 1) * PEER_NKEYS, :] = w

    @pl.when(jnp.logical_and(i > 0, j == 1))
    def _():
        y = acc_scr[(i + 1) % 2].T
        o_ref[...] = _layer_norm(alpha * xprev_ref[...] + y, ln_ref[0:1, :], ln_ref[1:2, :])


def _peer_ffn(x, wq_hi, wq_lo, keys, u_bf, vt_bf, ln, alpha):
    n = x.shape[0]
    n_exp = u_bf.shape[0]
    assert n % PEER_TN == 0 and n_exp == PEER_NKEYS * PEER_NKEYS and n_exp % PEER_TE == 0
    n_chunks = n_exp // PEER_TE
    q = _peer_query(x, wq_hi, wq_lo)
    nq = q.shape[1]
    head_scr = pltpu.VMEM((PEER_HEADS, PEER_NKEYS, PEER_TN), f32)
    head_scr16 = pltpu.VMEM((PEER_HEADS, PEER_NKEYS, PEER_TN), bf16)
    n_tiles = n // PEER_TN
    this_tile = lambda i, j: (jnp.minimum(i, n_tiles - 1), 0)
    out_tile = lambda i, j: (jnp.clip(jnp.where(j >= 2, i, i - 1), 0, n_tiles - 1), 0)
    return pl.pallas_call(
        functools.partial(_peer_kernel, alpha, n_tiles, n_chunks),
        grid=(n_tiles + 1, n_chunks),
        in_specs=[
            pl.BlockSpec((PEER_TN, D_MODEL), this_tile),
            pl.BlockSpec((PEER_TN, D_MODEL), lambda i, j: (jnp.maximum(i - 1, 0), 0)),
            pl.BlockSpec((PEER_TN, nq), this_tile),
            pl.BlockSpec((2 * PEER_HEADS, PEER_NKEYS, PEER_HALF), lambda i, j: (0, 0, 0)),
            pl.BlockSpec((PEER_TE, D_MODEL), lambda i, j: (j, 0)),
            pl.BlockSpec((D_MODEL, PEER_TE), lambda i, j: (0, (j + n_chunks - 2) % n_chunks)),
            pl.BlockSpec((2, D_MODEL), lambda i, j: (0, 0)),
        ],
        out_specs=pl.BlockSpec((PEER_TN, D_MODEL), out_tile),
        out_shape=jax.ShapeDtypeStruct((n, D_MODEL), f32),
        scratch_shapes=[pltpu.VMEM((D_MODEL, PEER_TN), bf16), head_scr, head_scr, head_scr16, head_scr16,
                        pltpu.VMEM((2, D_MODEL, PEER_TN), f32), pltpu.VMEM((2, PEER_TE, PEER_TN), bf16),
                        pltpu.VMEM((2, PEER_TE, PEER_TN), f32), pltpu.VMEM((2, PEER_TE, PEER_TN), bf16)],
        compiler_params=_cparams("arbitrary", "arbitrary"),
        name="peer_ffn",
    )(x, x, q, keys, u_bf, vt_bf, ln)


POOL_TM = 256


def kernel(x_prompt, x_sample, cache_k, cache_v, cache_logf, page_table, state_wkv, state_shift, state_pool,
           meta, w_in, b_f, w_o, mu_shift, w0, w2, a0, a2, g2, k_k, k_a, r_k, gn_g, gn_b, w_pool, pool_scale,
           ln_g, ln_b, peer_wq, peer_keys, peer_u, peer_v):
    prm = dict(mu_shift=mu_shift, w0=w0, w2=w2, a0=a0, a2=a2, g2=g2, k_k=k_k, k_a=k_a, r_k=r_k,
               gn_g=gn_g, gn_b=gn_b)
    bp, seq, d = x_prompt.shape
    db, dec_seq, _ = x_sample.shape
    assert bp == 1 and d == D_MODEL
    depth = ln_g.shape[0]
    alpha = (2 * depth) ** 0.25
    n_sample = db * dec_seq
    t_p = seq + N_META
    n_valid = n_sample + t_p
    n_rows = n_sample + -(-t_p // FOX_TQ) * FOX_TQ
    n_rows = -(-n_rows // ROW_TILE) * ROW_TILE
    h = jnp.concatenate([x_sample.reshape(n_sample, d), meta.astype(f32), x_prompt[0],
                         jnp.zeros((n_rows - n_valid, d), f32)], axis=0)

    ks, vs, lfs, wkvs, shifts, pools = ([[], []] for _ in range(6))
    n_qkv = A_IN + 3 * D_B
    for i in range(depth):
        j = i // 2
        ln1 = jnp.stack([ln_g[i, 0], ln_b[i, 0]])
        ln2 = jnp.stack([ln_g[i, 1], ln_b[i, 1]])
        if i % 2 == 0:
            w_cat = jnp.concatenate([w_in[j][:, :n_qkv],
                                     jnp.pad(w_in[j][:, n_qkv:], ((0, 0), (0, LANES - H_B)))], axis=1).astype(bf16)
            b_pad = jnp.pad(b_f[j], (0, LANES - H_B)).reshape(1, LANES)
            pa, kv, qkv16, lf = _in_projection(h, w_cat, b_pad)
            ya, wkv_p, wkv_s = _rwkv_mixer(pa, state_shift[j], state_wkv[j], prm, j, n_sample, n_valid, dec_seq)
            yb = _fox_attention(qkv16, kv, lf, cache_k[j], cache_v[j], cache_logf[j], page_table,
                                n_sample, n_valid, dec_seq)
            h1 = _out_projection(ya, yb, h, w_o[j].astype(bf16), ln1, alpha)
            for grp, lo, hi, lead in ((0, n_sample, n_valid, (bp, t_p)), (1, 0, n_sample, (db, dec_seq))):
                ks[grp].append(kv[lo:hi, :D_B].reshape(lead + (H_B, HEAD_DIM)))
                vs[grp].append(kv[lo:hi, D_B:].reshape(lead + (H_B, HEAD_DIM)))
                lfs[grp].append(lf[lo:hi, :H_B].reshape(lead + (H_B,)))
                shifts[grp].append(pa[lo:hi].reshape(lead + (A_IN,))[:, -1])
            wkvs[0].append(wkv_p)
            wkvs[1].append(wkv_s)
        else:
            ext_s = jnp.concatenate([jnp.zeros((db, POOL_HIST - POOL_BUF, d), f32), state_pool[j],
                                     h[:n_sample].reshape(db, dec_seq, d)], axis=1)
            pools[0].append(h[n_valid - POOL_BUF:n_valid][None])
            pools[1].append(ext_s[:, -POOL_BUF:])
            w_bf = w_pool[j].astype(bf16)
            scale = pool_scale[j].reshape(1, d)
            first = n_sample // POOL_TM
            hp = _pool_mixer(h, w_bf, scale, ln1, alpha, POOL_TM, first, n_rows // POOL_TM - first, True, n_sample)
            ext_len = ext_s.shape[1]
            hs = _pool_mixer(ext_s.reshape(db * ext_len, d), w_bf, scale, ln1, alpha, db * ext_len, 0, 1, False, 0)
            hs = hs.reshape(db, ext_len, d)[:, POOL_HIST:].reshape(n_sample, d)
            h1 = jnp.concatenate([hs, hp[n_sample:]], axis=0)
        wq = peer_wq[i]
        wq_hi = wq.astype(bf16)
        wq_lo = (wq - wq_hi.astype(f32)).astype(bf16)
        h = _peer_ffn(h1, wq_hi, wq_lo, peer_keys[i].reshape(2 * PEER_HEADS, PEER_NKEYS, PEER_HALF),
                      peer_u[i].astype(bf16), _transpose_bf16(peer_v[i]), ln2, alpha)

    y_prompt = h[n_sample + N_META:n_valid][None]
    y_sample = h[:n_sample].reshape(db, dec_seq, d)
    st = lambda xs: jnp.stack(xs)
    return (y_prompt, y_sample,
            st(ks[0]), st(vs[0]), st(lfs[0]), st(wkvs[0]), st(shifts[0]), st(pools[0]),
            st(ks[1]), st(vs[1]), st(lfs[1]), st(wkvs[1]), st(shifts[1]), st(pools[1]))
```

```python
import functools
import math

import jax
import jax.numpy as jnp
from jax import lax
from jax.experimental import pallas as pl
from jax.experimental.pallas import tpu as pltpu

f32 = jnp.float32
bf16 = jnp.bfloat16

D_MODEL = 1024
N_META = 16
HEAD_DIM = 64
D_A = D_MODEL // 2
D_B = D_MODEL - D_A
H_A = D_A // HEAD_DIM
H_B = D_B // HEAD_DIM
LORA_W = 64
LORA_A = 64
LORA_G = 128
A_K = D_A
A_V = 2 * D_A
A_W = 3 * D_A
A_IN = A_W + LORA_W + LORA_A + LORA_G
GN_EPS = HEAD_DIM * 1e-5
POOL_WINDOWS = (2, 4, 8, 16)
POOL_G = D_MODEL // len(POOL_WINDOWS)
POOL_BUF = max(POOL_WINDOWS) - 1
PEER_HEADS = 8
PEER_NKEYS = 128
PEER_TOPK = 16
PEER_HALF = 128
LN_EPS = 1e-5
NEG = -1e30

LANES = 128
PAIR = 2 * HEAD_DIM
N_PAIR = D_A // PAIR
ROW_TILE = 512
SCAN_C = 64
INV_BLOCK = 16
FOX_TQ = 256
FOX_TK = 256
PAGES_PER_STEP = 8
PEER_TN = 256
PEER_TE = 2048
VMEM_LIMIT = 56 * 1024 * 1024


def _cparams(*sem):
    return pltpu.CompilerParams(dimension_semantics=sem, vmem_limit_bytes=VMEM_LIMIT)


def _dot(a, b, dims=(((1,), (0,)), ((), ()))):
    return lax.dot_general(a, b, dims, preferred_element_type=f32)


NT = (((1,), (1,)), ((), ()))
TN = (((0,), (0,)), ((), ()))
NN = (((1,), (0,)), ((), ()))


def _split(x):
    hi = x.astype(bf16)
    lo = (x - hi.astype(f32)).astype(bf16)
    return hi, lo


def _dot3(a, b, dims=NN):
    ah, al = _split(a)
    bh, bl = _split(b)
    return _dot(ah, bh, dims) + (_dot(al, bh, dims) + _dot(ah, bl, dims))


def _dot_exact_rhs(a, b_bf, dims=NN):
    a0 = a.astype(bf16)
    r1 = a - a0.astype(f32)
    a1 = r1.astype(bf16)
    a2 = (r1 - a1.astype(f32)).astype(bf16)
    return _dot(a0, b_bf, dims) + (_dot(a1, b_bf, dims) + _dot(a2, b_bf, dims))


def _dot_exact_lhs(a_bf, b, dims=NN):
    b0 = b.astype(bf16)
    r1 = b - b0.astype(f32)
    b1 = r1.astype(bf16)
    b2 = (r1 - b1.astype(f32)).astype(bf16)
    return _dot(a_bf, b0, dims) + (_dot(a_bf, b1, dims) + _dot(a_bf, b2, dims))


def _layer_norm(x, g, b):
    mu = jnp.mean(x, axis=-1, keepdims=True)
    xc = x - mu
    var = jnp.mean(xc * xc, axis=-1, keepdims=True)
    return xc * lax.rsqrt(var + LN_EPS) * g + b


def _head_ones():
    r = lax.broadcasted_iota(jnp.int32, (D_A, D_A), 0) // HEAD_DIM
    c = lax.broadcasted_iota(jnp.int32, (D_A, D_A), 1) // HEAD_DIM
    return jnp.where(r == c, 1.0, 0.0).astype(bf16)


def _proj_kernel(x_ref, w_ref, bf_ref, pa_ref, kv_ref, qkv16_ref, lf_ref):
    x = x_ref[...].astype(bf16)
    y = _dot(x, w_ref[...])
    pa_ref[...] = y[:, :A_IN]
    q = y[:, A_IN:A_IN + D_B]
    kv = y[:, A_IN + D_B:A_IN + 3 * D_B]
    kv_ref[...] = kv
    qkv16_ref[:, :D_B] = (q * (HEAD_DIM ** -0.5)).astype(bf16)
    qkv16_ref[:, D_B:] = kv.astype(bf16)
    lf_ref[...] = jax.nn.log_sigmoid(y[:, A_IN + 3 * D_B:] + bf_ref[...])


def _in_projection(x, w_bf, b_f_pad):
    n = x.shape[0]
    nw = w_bf.shape[1]
    return pl.pallas_call(
        _proj_kernel,
        grid=(n // ROW_TILE,),
        in_specs=[
            pl.BlockSpec((ROW_TILE, D_MODEL), lambda i: (i, 0)),
            pl.BlockSpec((D_MODEL, nw), lambda i: (0, 0)),
            pl.BlockSpec((1, LANES), lambda i: (0, 0)),
        ],
        out_specs=[
            pl.BlockSpec((ROW_TILE, A_IN), lambda i: (i, 0)),
            pl.BlockSpec((ROW_TILE, 2 * D_B), lambda i: (i, 0)),
            pl.BlockSpec((ROW_TILE, 3 * D_B), lambda i: (i, 0)),
            pl.BlockSpec((ROW_TILE, LANES), lambda i: (i, 0)),
        ],
        out_shape=[
            jax.ShapeDtypeStruct((n, A_IN), f32),
            jax.ShapeDtypeStruct((n, 2 * D_B), f32),
            jax.ShapeDtypeStruct((n, 3 * D_B), bf16),
            jax.ShapeDtypeStruct((n, LANES), f32),
        ],
        compiler_params=_cparams("parallel"),
        name="in_projection",
    )(x, w_bf, b_f_pad)


PREP_TM = 256


def _prep_kernel(n_sample, n_valid, dec_seq,
                 pa_ref, prev_ref, ss_ref, mu_ref, vec_ref, w2_ref, a2_ref, g2_ref,
                 r_o, k_o, v_o, kk_o, b_o, lw_o, g_o, bonus_o):
    i = pl.program_id(0)
    tm = pa_ref.shape[0]
    pa = pa_ref[...]
    local = lax.broadcasted_iota(jnp.int32, (tm, 1), 0)
    row = i * tm + local
    prev = pltpu.roll(pa, 1, 0)
    prev = jnp.where(local == 0, prev_ref[7:8, :], prev)
    prev = jnp.where(row == n_sample, 0.0, prev)
    seq_start = jnp.logical_and(row < n_sample, row % dec_seq == 0)
    prev = jnp.where(seq_start, ss_ref[...], prev)
    xs = pa + (prev - pa) * mu_ref[...]

    w0 = vec_ref[0:1, :]
    a0 = vec_ref[1:2, :]
    k_k = vec_ref[2:3, :]
    k_a = vec_ref[3:4, :]
    r_k = vec_ref[4:5, :]
    r = xs[:, :A_K]
    k = xs[:, A_K:A_V]
    v = xs[:, A_V:A_W]
    wa = xs[:, A_W:A_W + LORA_W + LORA_A]
    xg = xs[:, A_W + LORA_W + LORA_A:]
    w = -jax.nn.softplus(-(w0 + _dot3(jnp.tanh(wa), w2_ref[...]))) - 0.5
    lw = -jnp.exp(w)
    a = jax.nn.sigmoid(a0 + _dot3(wa, a2_ref[...]))
    g = _dot(jax.nn.sigmoid(xg).astype(bf16), g2_ref[...].astype(bf16))
    ones = _head_ones()
    kk = k * k_k
    kk = kk / jnp.maximum(jnp.sqrt(_dot_exact_rhs(kk * kk, ones)), 1e-12)
    k = k * (1.0 + (a - 1.0) * k_a)
    bonus = _dot_exact_rhs(r * k * r_k, ones) * v
    valid = row < n_valid
    r_o[...] = r
    k_o[...] = jnp.where(valid, k, 0.0)
    v_o[...] = jnp.where(valid, v, 0.0)
    kk_o[...] = kk
    b_o[...] = jnp.where(valid, kk * a, 0.0)
    lw_o[...] = jnp.where(valid, lw, 0.0)
    g_o[...] = g
    bonus_o[...] = bonus


def _rwkv_prep(pa, ss_rows, mu, vecs, w2p, a2p, g2, n_sample, n_valid, dec_seq):
    n = pa.shape[0]
    assert n_sample == PREP_TM and n % PREP_TM == 0
    row_spec = pl.BlockSpec((PREP_TM, D_A), lambda i: (i, 0))
    const = lambda shape: pl.BlockSpec(shape, lambda i: (0, 0))
    return pl.pallas_call(
        functools.partial(_prep_kernel, n_sample, n_valid, dec_seq),
        grid=(n // PREP_TM,),
        in_specs=[
            pl.BlockSpec((PREP_TM, A_IN), lambda i: (i, 0)),
            pl.BlockSpec((8, A_IN), lambda i: (jnp.maximum(i * (PREP_TM // 8) - 1, 0), 0)),
            const((PREP_TM, A_IN)),
            const((1, A_IN)),
            const((8, D_A)),
            const((LORA_W + LORA_A, D_A)),
            const((LORA_W + LORA_A, D_A)),
            const((LORA_G, D_A)),
        ],
        out_specs=[row_spec] * 8,
        out_shape=[jax.ShapeDtypeStruct((n, D_A), f32)] * 8,
        compiler_params=_cparams("parallel"),
        name="rwkv_prep",
    )(pa, pa, ss_rows, mu, vecs, w2p, a2p, g2)


def _stack_heads(x, lane_a):
    return jnp.concatenate([jnp.where(lane_a, x, 0.0), jnp.where(lane_a, 0.0, x)], axis=0)


def _unit_lower_inverse(x):
    n = x.shape[0]
    ri = lax.broadcasted_iota(jnp.int32, (n, n), 0)
    ci = lax.broadcasted_iota(jnp.int32, (n, n), 1)
    eye = jnp.where(ri == ci, 1.0, 0.0)
    d = jnp.where(ri // INV_BLOCK == ci // INV_BLOCK, x, 0.0)
    off = x - d
    d2 = _dot3(d, d)
    yield
    d4 = _dot3(d2, d2)
    yield
    d8 = _dot3(d4, d4)
    p = eye - d
    p = p + _dot3(p, d2)
    yield
    p = p + _dot3(p, d4)
    yield
    dinv = p + _dot3(p, d8)
    yield
    m = _dot3(dinv, off)
    yield
    m2 = _dot3(m, m)
    yield
    q = eye - m
    q = q + _dot3(q, m2)
    yield
    return _dot3(q, dinv)


def _scan_pair(r, k, v, kk, b, lw, g, bonus, gn, get_state):
    C = r.shape[0]
    assert C == SCAN_C and SCAN_C // INV_BLOCK == 4 and 2 * C == PAIR
    lane = lax.broadcasted_iota(jnp.int32, (1, PAIR), 1)
    lane_a = lane < HEAD_DIM
    t_idx = lax.broadcasted_iota(jnp.int32, (C, 1), 0)
    tri = jnp.where(lax.broadcasted_iota(jnp.int32, (C, C), 1) <= t_idx, 1.0, 0.0).astype(bf16)
    cs = _dot_exact_lhs(tri, lw)
    total = cs[C - 1:C, :]
    kp = kk * jnp.exp(cs - lw)
    rp = r * jnp.exp(cs)
    g_inv = jnp.exp(-cs)
    g_tail = jnp.exp(total - cs)
    kt, bt = k * g_inv, b * g_inv
    kh, bh = k * g_tail, b * g_tail

    st = lambda x: _stack_heads(x, lane_a)
    sc = _dot3(jnp.concatenate([kp, rp], axis=0),
               jnp.concatenate([st(bt), st(kt)], axis=0), NT)
    yield
    s_idx = lane % C
    strict = s_idx < t_idx
    incl = s_idx <= t_idx
    lb = jnp.where(strict, sc[:C, :PAIR], 0.0)
    lk = jnp.where(strict, sc[:C, PAIR:], 0.0)
    pb = jnp.where(incl, sc[C:, :PAIR], 0.0)
    pk = jnp.where(incl, sc[C:, PAIR:], 0.0)

    lkv = _dot3(lk, st(v))
    yield
    tinv = yield from _unit_lower_inverse(st(lb))
    yield
    tcat = tinv[:C] + tinv[C:]
    w12 = _dot3(tcat, jnp.concatenate([st(lkv), st(kp)], axis=1))
    yield
    w1, w2 = w12[:, :PAIR], w12[:, PAIR:]
    pbw = _dot(pb.astype(bf16), jnp.concatenate([st(w1), st(w2)], axis=1).astype(bf16))
    pkv = _dot(pk.astype(bf16), st(v).astype(bf16))
    yield
    S = get_state()
    q2 = rp - pbw[:, PAIR:]
    y = _dot(q2.astype(bf16), S.astype(bf16), NT) + (pkv - pbw[:, :PAIR])

    ri = lax.broadcasted_iota(jnp.int32, (PAIR, PAIR), 0)
    ci = lax.broadcasted_iota(jnp.int32, (PAIR, PAIR), 1)
    same_head = (ri // HEAD_DIM) == (ci // HEAD_DIM)
    wb = _dot3(jnp.concatenate([w2, w1], axis=1), bh, TN)
    yield
    g2m = jnp.where(ri == ci, jnp.exp(total), 0.0) - jnp.where(same_head, wb[:PAIR], 0.0)
    g1m = jnp.where(same_head, _dot3(v, kh, TN) - wb[PAIR:], 0.0)
    yield
    s_new = _dot3(S, g2m) + g1m
    yield

    ones = jnp.where(same_head, 1.0, 0.0).astype(bf16)
    mean = _dot_exact_rhs(y, ones) * (1.0 / HEAD_DIM)
    yc = y - mean
    var = _dot_exact_rhs(yc * yc, ones) * (1.0 / HEAD_DIM)
    yn = yc * lax.rsqrt(var + GN_EPS) * gn[0:1, :] + gn[1:2, :]
    return (yn + bonus) * g, s_new


SCAN_STEP_CHUNKS = 2
SCAN_LAG = 4


def _scan_kernel(per_chunk_state, *refs):
    if per_chunk_state:
        (r_ref, k_ref, v_ref, kk_ref, b_ref, lw_ref, g_ref, bonus_ref, gn_ref, s0_ref,
         ya_ref, sout_ref, s_scr) = refs
    else:
        (r_ref, k_ref, v_ref, kk_ref, b_ref, lw_ref, g_ref, bonus_ref, gn_ref,
         ya_ref, sout_ref, s_scr) = refs

        @pl.when(pl.program_id(0) == 0)
        def _():
            s_scr[...] = jnp.zeros(s_scr.shape, f32)

    row_refs = (r_ref, k_ref, v_ref, kk_ref, b_ref, lw_ref, g_ref, bonus_ref)
    done = {}

    def incoming(cc, p):
        if per_chunk_state:
            return lambda: s0_ref[cc, p]
        return (lambda: s_scr[p]) if cc == 0 else (lambda: done[cc - 1, p][1])

    chains = {}
    for cc in range(SCAN_STEP_CHUNKS):
        rows = slice(cc * SCAN_C, (cc + 1) * SCAN_C)
        for p in range(N_PAIR):
            sl = slice(p * PAIR, (p + 1) * PAIR)
            chains[cc, p] = _scan_pair(*(x[rows, sl] for x in row_refs), gn_ref[:, sl], incoming(cc, p))
    rnd = 0
    while chains:
        for (cc, p) in sorted(chains):
            if rnd < cc * SCAN_LAG:
                continue
            try:
                next(chains[cc, p])
            except StopIteration as stop:
                ya, s_new = done[cc, p] = stop.value
                ya_ref[cc * SCAN_C:(cc + 1) * SCAN_C, p * PAIR:(p + 1) * PAIR] = ya
                if per_chunk_state:
                    sout_ref[cc, p] = s_new
                elif cc == SCAN_STEP_CHUNKS - 1:
                    s_scr[p] = s_new
                    sout_ref[0, p] = s_new
                del chains[cc, p]
        rnd += 1


def _rwkv_scan(arrs, gn, n_rows_out, row_block_off, n_chunks, s0=None):
    per_chunk = s0 is not None
    assert n_chunks % SCAN_STEP_CHUNKS == 0
    blk = pl.BlockSpec((SCAN_STEP_CHUNKS * SCAN_C, D_A), lambda c: (c + row_block_off, 0))
    in_specs = [blk] * 8 + [pl.BlockSpec((2, D_A), lambda c: (0, 0))]
    args = list(arrs) + [gn]
    if per_chunk:
        n_states, state_block = n_chunks, SCAN_STEP_CHUNKS
        s_map = lambda c: (c, 0, 0, 0)
        in_specs.append(pl.BlockSpec((state_block, N_PAIR, PAIR, PAIR), s_map))
        args.append(s0)
    else:
        n_states, state_block = 1, 1
        s_map = lambda c: (0, 0, 0, 0)
    return pl.pallas_call(
        functools.partial(_scan_kernel, per_chunk),
        grid=(n_chunks // SCAN_STEP_CHUNKS,),
        in_specs=in_specs,
        out_specs=[blk, pl.BlockSpec((state_block, N_PAIR, PAIR, PAIR), s_map)],
        out_shape=[jax.ShapeDtypeStruct((n_rows_out, D_A), f32),
                   jax.ShapeDtypeStruct((n_states, N_PAIR, PAIR, PAIR), f32)],
        scratch_shapes=[pltpu.VMEM((N_PAIR, PAIR, PAIR), f32)],
        compiler_params=_cparams("arbitrary"),
        name="rwkv_scan",
    )(*args)


def _pair_states(state):
    b = state.shape[0]
    s = state.reshape(b, N_PAIR, 2, HEAD_DIM, HEAD_DIM)
    out = jnp.zeros((b, N_PAIR, 2, HEAD_DIM, 2, HEAD_DIM), f32)
    out = out.at[:, :, 0, :, 0, :].set(s[:, :, 0]).at[:, :, 1, :, 1, :].set(s[:, :, 1])
    return out.reshape(b, N_PAIR, PAIR, PAIR)


def _unpair_states(s):
    b = s.shape[0]
    s = s.reshape(b, N_PAIR, 2, HEAD_DIM, 2, HEAD_DIM)
    return jnp.stack([s[:, :, 0, :, 0, :], s[:, :, 1, :, 1, :]], axis=2).reshape(b, H_A, HEAD_DIM, HEAD_DIM)


def _rwkv_mixer(pa, state_shift, state_wkv, prm, j, n_sample, n_valid, dec_seq):
    n = pa.shape[0]
    db = n_sample // dec_seq
    ss_rows = jnp.zeros((db, dec_seq, A_IN), f32).at[:, 0].set(state_shift).reshape(n_sample, A_IN)
    vecs = jnp.zeros((8, D_A), f32)
    for idx, name in enumerate(("w0", "a0", "k_k", "k_a", "r_k")):
        vecs = vecs.at[idx].set(prm[name][j].reshape(D_A))
    zpad = jnp.zeros((LORA_W, D_A), f32)
    w2p = jnp.concatenate([prm["w2"][j], zpad], axis=0)
    a2p = jnp.concatenate([zpad, prm["a2"][j]], axis=0)
    arrs = _rwkv_prep(pa, ss_rows, prm["mu_shift"][j].reshape(1, A_IN), vecs, w2p, a2p, prm["g2"][j],
                      n_sample, n_valid, dec_seq)
    gn = jnp.stack([prm["gn_g"][j], prm["gn_b"][j]])
    step_rows = SCAN_STEP_CHUNKS * SCAN_C
    n_chunks = -(-(n_valid - n_sample) // step_rows) * SCAN_STEP_CHUNKS
    assert n_sample % step_rows == 0 and n_sample + n_chunks * SCAN_C <= n and dec_seq <= SCAN_C
    assert db % SCAN_STEP_CHUNKS == 0
    ya_p, s_p = _rwkv_scan(arrs, gn, n, n_sample // step_rows, n_chunks)

    def pad_seq(x):
        x = x[:n_sample].reshape(db, dec_seq, D_A)
        return jnp.pad(x, ((0, 0), (0, SCAN_C - dec_seq), (0, 0))).reshape(db * SCAN_C, D_A)

    ya_s, s_s = _rwkv_scan([pad_seq(x) for x in arrs], gn, db * SCAN_C, 0, db, s0=_pair_states(state_wkv))
    ya_s = ya_s.reshape(db, SCAN_C, D_A)[:, :dec_seq].reshape(n_sample, D_A)
    ya = jnp.concatenate([ya_s, ya_p[n_sample:]], axis=0)
    return ya, _unpair_states(s_p[0][None]), _unpair_states(s_s)


CUM_BLOCK = 256


def _upper_ones(n):
    r = lax.broadcasted_iota(jnp.int32, (n, n), 0)
    c = lax.broadcasted_iota(jnp.int32, (n, n), 1)
    return jnp.where(r <= c, 1.0, 0.0).astype(bf16)


def _cumsum_kernel(x_ref, d_ref, c_ref, carry):
    @pl.when(pl.program_id(0) == 0)
    def _():
        carry[...] = jnp.zeros_like(carry)

    x = x_ref[...]
    n = x.shape[1]
    d_ref[...] = _dot_exact_rhs(x, _upper_ones(n))
    c_ref[0] = carry[...]
    carry[...] = carry[...] + _dot_exact_rhs(x, jnp.ones((n, LANES), bf16))


def _logf_cumsum(lf_t, first_block):
    h, n = lf_t.shape
    nb = n // CUM_BLOCK
    return pl.pallas_call(
        _cumsum_kernel,
        grid=(nb - first_block,),
        in_specs=[pl.BlockSpec((h, CUM_BLOCK), lambda i: (0, i + first_block))],
        out_specs=[pl.BlockSpec((h, CUM_BLOCK), lambda i: (0, i + first_block)),
                   pl.BlockSpec((1, h, LANES), lambda i: (i + first_block, 0, 0))],
        out_shape=[jax.ShapeDtypeStruct((h, n), f32), jax.ShapeDtypeStruct((nb, h, LANES), f32)],
        scratch_shapes=[pltpu.VMEM((h, LANES), f32)],
        compiler_params=_cparams("arbitrary"),
        name="logf_cumsum",
    )(lf_t)


FOX_KW = 2 * PAIR


def _fox_prompt_kernel(row0, cref_ref, q_ref, k_ref, va_ref, vb_ref, o_ref, s_scr, acc_scr):
    pair = pl.program_id(0)
    i = pl.program_id(1)
    q = q_ref[...]
    lane = lax.broadcasted_iota(jnp.int32, (FOX_TQ, PAIR), 1)
    lane_a = lane < HEAD_DIM
    zero = jnp.zeros_like(q)

    def query_t(qh, col):
        bias = jnp.where(jnp.logical_or(lane == col, lane == col + 1), -1.0, 0.0)
        qa = jnp.concatenate([qh.astype(f32), bias], axis=1)
        return qa.T.astype(bf16)

    q_t = (query_t(jnp.where(lane_a, q, zero), 0), query_t(jnp.where(lane_a, zero, q), 2))
    v_refs = (va_ref, vb_ref)
    key_idx = lax.broadcasted_iota(jnp.int32, (FOX_TK, 1), 0)
    qry_idx = lax.broadcasted_iota(jnp.int32, (1, FOX_TQ), 1)
    blk0 = row0 // FOX_TK

    def block_start(j):
        return pl.multiple_of(row0 + j * FOX_TK, FOX_TK)

    def scores(j):
        kb = k_ref[pl.ds(block_start(j), FOX_TK), :]
        return tuple(_dot(kb, q_t[hh]) for hh in range(2))

    def block_max(j, s_pair):
        return tuple(jnp.max(s_pair[hh], axis=0, keepdims=True) - cref_ref[(blk0 + j) * H_B + 2 * pair + hh]
                     for hh in range(2))

    def update(j, s_pair, mt_pair, ms):
        ps, ms_new, alphas = [], [], []
        for hh in range(2):
            cref = cref_ref[(blk0 + j) * H_B + 2 * pair + hh]
            m_new = jnp.maximum(ms[hh], mt_pair[hh])
            ps.append(jnp.exp((s_pair[hh] - (m_new + cref)).astype(bf16)))
            ms_new.append(m_new)
            alphas.append(jnp.exp(ms[hh] - m_new))
        start = block_start(j)
        for hh in range(2):
            acc_scr[hh] = acc_scr[hh] * alphas[hh] + _dot(v_refs[hh][pl.ds(start, FOX_TK), :], ps[hh], TN)
        return tuple(ms_new)

    def step(j, state):
        ms, mt_cur = state
        s_cur = (s_scr[0], s_scr[1])
        s_next = scores(j + 1)
        ms = update(j, s_cur, mt_cur, ms)
        s_scr[0], s_scr[1] = s_next
        return ms, block_max(j + 1, s_next)

    neg = jnp.full((1, FOX_TQ), NEG, f32)
    acc_scr[...] = jnp.zeros(acc_scr.shape, f32)
    s0 = scores(0)
    s_scr[0], s_scr[1] = s0
    ms, _ = lax.fori_loop(0, i, step, ((neg, neg), block_max(0, s0)))
    s_last = tuple(jnp.where(key_idx <= qry_idx, s_scr[hh], NEG) for hh in range(2))
    update(i, s_last, block_max(i, s_last), ms)
    acc_a, acc_b = acc_scr[0], acc_scr[1]
    out_t = jnp.concatenate([acc_a[:HEAD_DIM] / acc_a[HEAD_DIM:HEAD_DIM + 1],
                             acc_b[HEAD_DIM:] / acc_b[0:1]], axis=0)
    o_ref[...] = out_t.T


def _fox_prompt(q16, k_aug, v_a, v_b, cref, n_sample, n_valid):
    n = q16.shape[0]
    assert FOX_TQ == FOX_TK == CUM_BLOCK and n_sample % FOX_TQ == 0
    nq = -(-(n_valid - n_sample) // FOX_TQ)
    assert n_sample + nq * FOX_TQ <= n
    qb0 = n_sample // FOX_TQ
    nb = D_B // PAIR
    return pl.pallas_call(
        functools.partial(_fox_prompt_kernel, n_sample),
        grid=(nb, nq),
        in_specs=[
            pl.BlockSpec(memory_space=pltpu.SMEM),
            pl.BlockSpec((FOX_TQ, PAIR), lambda p, i: (i + qb0, p)),
            pl.BlockSpec((n, FOX_KW), lambda p, i: (0, p)),
            pl.BlockSpec((n, PAIR), lambda p, i: (0, p)),
            pl.BlockSpec((n, PAIR), lambda p, i: (0, p)),
        ],
        out_specs=pl.BlockSpec((FOX_TQ, PAIR), lambda p, i: (i + qb0, p)),
        out_shape=jax.ShapeDtypeStruct((n, D_B), f32),
        scratch_shapes=[pltpu.VMEM((2, FOX_TK, FOX_TQ), f32), pltpu.VMEM((2, PAIR, FOX_TQ), f32)],
        compiler_params=_cparams("parallel", "arbitrary"),
        name="fox_prompt",
    )(cref, q16, k_aug, v_a, v_b)


def _fox_sample_kernel(n_steps, page_size, pt_ref, q_ref, kn_ref, vn_ref, lfn_ref, *refs):
    np_ = PAGES_PER_STEP
    k_refs, v_refs, lf_refs = refs[:np_], refs[np_:2 * np_], refs[2 * np_:3 * np_]
    o_ref, m_scr, l_scr, acc_scr, carry_scr = refs[3 * np_:]
    j = pl.program_id(1)
    ds = q_ref.shape[1]
    nrow = H_B * ds

    @pl.when(j == 0)
    def _():
        m_scr[...] = jnp.full(m_scr.shape, NEG, f32)
        l_scr[...] = jnp.zeros(l_scr.shape, f32)
        acc_scr[...] = jnp.zeros(acc_scr.shape, f32)
        carry_scr[...] = jnp.zeros(carry_scr.shape, f32)

    row_h = lax.broadcasted_iota(jnp.int32, (nrow, D_B), 0) // ds
    lane_h = lax.broadcasted_iota(jnp.int32, (nrow, D_B), 1) // HEAD_DIM
    head_sel = row_h == lane_h
    q = q_ref[0]
    qx = jnp.where(head_sel, jnp.broadcast_to(q[None], (H_B, ds, D_B)).reshape(nrow, D_B), 0.0).astype(bf16)

    def attend(kts, vts, lfs, mask):
        nk = page_size * len(kts)
        cum = lfs[0] if len(lfs) == 1 else jnp.concatenate(lfs, axis=1)
        key_lane = lax.broadcasted_iota(jnp.int32, (1, nk), 1)
        shift = 1
        while shift < nk:
            cum = cum + jnp.where(key_lane >= shift, pltpu.roll(cum, shift, 1), 0.0)
            shift *= 2
        run = carry_scr[...]
        carry_scr[...] = run + cum[:, nk - 1:nk]
        cum = cum + run
        cat = lambda xs: (xs[0] if len(xs) == 1 else jnp.concatenate(xs, axis=1)).astype(bf16)
        s = _dot(qx, cat(kts))
        s = s - jnp.broadcast_to(cum[:, None, :], (H_B, ds, nk)).reshape(nrow, nk)
        if mask is not None:
            s = jnp.where(mask, s, NEG)
        m = m_scr[...]
        m_new = jnp.maximum(m, jnp.max(s, axis=1, keepdims=True))
        alpha = jnp.exp(m - m_new)
        p = jnp.exp(s - m_new)
        l_scr[...] = l_scr[...] * alpha + jnp.sum(p, axis=1, keepdims=True)
        acc_scr[...] = acc_scr[...] * alpha + _dot(p.astype(bf16), cat(vts), NT)
        m_scr[...] = m_new

    attend([r[0] for r in k_refs], [r[0] for r in v_refs], [r[0] for r in lf_refs], None)

    @pl.when(j == n_steps - 1)
    def _():
        t_q = lax.broadcasted_iota(jnp.int32, (nrow, page_size), 0) % ds
        t_k = lax.broadcasted_iota(jnp.int32, (nrow, page_size), 1)
        attend([kn_ref[0]], [vn_ref[0]], [lfn_ref[0]], t_k <= t_q)
        o = jnp.where(head_sel, acc_scr[...] / l_scr[...], 0.0)
        out = o[0:ds]
        for h in range(1, H_B):
            out = out + o[h * ds:(h + 1) * ds]
        o_ref[...] = out


def _fox_sample(q_s, kn_t, vn_t, lfn_t, cache_kt, cache_vt, cache_lft, page_table, dec_seq):
    db = q_s.shape[0]
    n_pages = page_table.shape[1]
    page_size = cache_kt.shape[2]
    assert n_pages % PAGES_PER_STEP == 0 and page_size == LANES
    n_steps = n_pages // PAGES_PER_STEP

    def page_map(pp):
        return lambda b, j, pt: (pt[b, j * PAGES_PER_STEP + pp], 0, 0)

    batch3 = lambda shape: pl.BlockSpec((1,) + shape, lambda b, j, pt: (b, 0, 0))
    in_specs = [batch3((dec_seq, D_B)), batch3((D_B, page_size)), batch3((D_B, page_size)),
                batch3((H_B, page_size))]
    in_specs += [pl.BlockSpec((1, D_B, page_size), page_map(pp)) for pp in range(PAGES_PER_STEP)]
    in_specs += [pl.BlockSpec((1, D_B, page_size), page_map(pp)) for pp in range(PAGES_PER_STEP)]
    in_specs += [pl.BlockSpec((1, H_B, page_size), page_map(pp)) for pp in range(PAGES_PER_STEP)]
    nrow = H_B * dec_seq
    grid_spec = pltpu.PrefetchScalarGridSpec(
        num_scalar_prefetch=1,
        grid=(db, n_steps),
        in_specs=in_specs,
        out_specs=pl.BlockSpec((dec_seq, D_B), lambda b, j, pt: (b, 0)),
        scratch_shapes=[pltpu.VMEM((nrow, 1), f32), pltpu.VMEM((nrow, 1), f32),
                        pltpu.VMEM((nrow, D_B), f32), pltpu.VMEM((H_B, 1), f32)],
    )
    return pl.pallas_call(
        functools.partial(_fox_sample_kernel, n_steps, page_size),
        grid_spec=grid_spec,
        out_shape=jax.ShapeDtypeStruct((db * dec_seq, D_B), f32),
        compiler_params=_cparams("parallel", "arbitrary"),
        name="fox_sample",
    )(page_table, q_s, kn_t, vn_t, lfn_t, *([cache_kt] * PAGES_PER_STEP), *([cache_vt] * PAGES_PER_STEP),
      *([cache_lft] * PAGES_PER_STEP))


def _fox_attention(qkv16, kv, lf, cache_k, cache_v, cache_lf, page_table, n_sample, n_valid, dec_seq):
    n = qkv16.shape[0]
    db = n_sample // dec_seq
    lf_t = lf[:, :H_B].T
    delta_t, c_blk = _logf_cumsum(lf_t, n_sample // CUM_BLOCK)
    delta = delta_t.T
    d_hi = delta.astype(bf16)
    d_lo = (delta - d_hi.astype(f32)).astype(bf16)
    k16, v16 = qkv16[:, D_B:2 * D_B], qkv16[:, 2 * D_B:]
    zpad = jnp.zeros((n, FOX_KW - PAIR - 4), bf16)
    parts = []
    for p in range(D_B // PAIR):
        parts += [k16[:, p * PAIR:(p + 1) * PAIR], d_hi[:, 2 * p:2 * p + 1], d_lo[:, 2 * p:2 * p + 1],
                  d_hi[:, 2 * p + 1:2 * p + 2], d_lo[:, 2 * p + 1:2 * p + 2], zpad]
    k_aug = jnp.concatenate(parts, axis=1)
    lane_a = (jnp.arange(D_B) % PAIR) < HEAD_DIM
    one = jnp.ones((), bf16)
    yb_p = _fox_prompt(qkv16, k_aug, jnp.where(lane_a, v16, one), jnp.where(lane_a, one, v16),
                       c_blk[:, :, 0].reshape(-1), n_sample, n_valid)
    n_pool, page_size = cache_k.shape[0], cache_k.shape[1]
    q_s = qkv16[:n_sample, :D_B].astype(f32).reshape(db, dec_seq, D_B)
    pad_keys = lambda x: jnp.pad(jnp.swapaxes(x, 1, 2), ((0, 0), (0, 0), (0, page_size - dec_seq)))
    kn_t = pad_keys(kv[:n_sample, :D_B].reshape(db, dec_seq, D_B))
    vn_t = pad_keys(kv[:n_sample, D_B:].reshape(db, dec_seq, D_B))
    lfn_t = pad_keys(lf[:n_sample, :H_B].reshape(db, dec_seq, H_B))
    to_t = lambda c: jnp.transpose(c, (0, 2, 3, 1)).reshape(n_pool, D_B, page_size)
    yb_s = _fox_sample(q_s, kn_t, vn_t, lfn_t, to_t(cache_k), to_t(cache_v), jnp.swapaxes(cache_lf, 1, 2),
                       page_table, dec_seq)
    return jnp.concatenate([yb_s, yb_p[n_sample:]], axis=0)


def _outproj_kernel(alpha, ya_ref, yb_ref, h_ref, w_ref, ln_ref, o_ref):
    y = _dot(ya_ref[...].astype(bf16), w_ref[:D_A, :]) + _dot(yb_ref[...].astype(bf16), w_ref[D_A:, :])
    o_ref[...] = _layer_norm(alpha * h_ref[...] + y, ln_ref[0:1, :], ln_ref[1:2, :])


def _out_projection(ya, yb, h, w_bf, ln, alpha):
    n = h.shape[0]
    half = pl.BlockSpec((ROW_TILE, D_A), lambda i: (i, 0))
    full = pl.BlockSpec((ROW_TILE, D_MODEL), lambda i: (i, 0))
    return pl.pallas_call(
        functools.partial(_outproj_kernel, alpha),
        grid=(n // ROW_TILE,),
        in_specs=[half, half, full,
                  pl.BlockSpec((D_MODEL, D_MODEL), lambda i: (0, 0)),
                  pl.BlockSpec((2, D_MODEL), lambda i: (0, 0))],
        out_specs=full,
        out_shape=jax.ShapeDtypeStruct((n, D_MODEL), f32),
        compiler_params=_cparams("parallel"),
        name="out_projection",
    )(ya, yb, h, w_bf, ln)


POOL_HIST = POOL_BUF + 1


def _pool_kernel(prompt_mode, zero_hist_tile, row0, alpha, x_ref, hist_ref, w_ref, sc_ref, ln_ref, o_ref):
    i = pl.program_id(0) + zero_hist_tile
    tm = x_ref.shape[0]
    x = x_ref[...]
    hist = jnp.where(i == zero_hist_tile, 0.0, hist_ref[...])
    ext = jnp.concatenate([hist, x], axis=0)
    t = i * tm + lax.broadcasted_iota(jnp.int32, (tm, 1), 0) - row0
    ys = []
    for gi, win in enumerate(POOL_WINDOWS):
        s = ext[:, gi * POOL_G:(gi + 1) * POOL_G]
        span = 1
        while span < win:
            s = s + pltpu.roll(s, span, 0)
            span *= 2
        cnt = jnp.minimum(win, t + 1).astype(f32) if prompt_mode else float(win)
        diff = s[POOL_HIST:] / cnt - x[:, gi * POOL_G:(gi + 1) * POOL_G]
        ys.append(_dot(diff.astype(bf16), w_ref[gi]))
    y = jnp.concatenate(ys, axis=1) * sc_ref[...]
    o_ref[...] = _layer_norm(alpha * x + y, ln_ref[0:1, :], ln_ref[1:2, :])


def _pool_mixer(x, w_bf, scale, ln, alpha, tm, first_tile, n_tiles, prompt_mode, row0):
    n = x.shape[0]
    assert tm % POOL_HIST == 0
    hb = tm // POOL_HIST
    return pl.pallas_call(
        functools.partial(_pool_kernel, prompt_mode, first_tile, row0, alpha),
        grid=(n_tiles,),
        in_specs=[
            pl.BlockSpec((tm, D_MODEL), lambda i: (i + first_tile, 0)),
            pl.BlockSpec((POOL_HIST, D_MODEL), lambda i: (jnp.maximum((i + first_tile) * hb - 1, 0), 0)),
            pl.BlockSpec((len(POOL_WINDOWS), POOL_G, POOL_G), lambda i: (0, 0, 0)),
            pl.BlockSpec((1, D_MODEL), lambda i: (0, 0)),
            pl.BlockSpec((2, D_MODEL), lambda i: (0, 0)),
        ],
        out_specs=pl.BlockSpec((tm, D_MODEL), lambda i: (i + first_tile, 0)),
        out_shape=jax.ShapeDtypeStruct((n, D_MODEL), f32),
        compiler_params=_cparams("parallel"),
        name="pool_mixer",
    )(x, x, w_bf, scale, ln)


def _peer_query_kernel(x_ref, wh_ref, wl_ref, o_ref):
    xh, xl = _split(x_ref[...])
    o_ref[...] = _dot(xh, wh_ref[...]) + (_dot(xl, wh_ref[...]) + _dot(xh, wl_ref[...]))


def _peer_query(x, wq_hi, wq_lo):
    n = x.shape[0]
    nq = wq_hi.shape[1]
    return pl.pallas_call(
        _peer_query_kernel,
        grid=(n // ROW_TILE,),
        in_specs=[pl.BlockSpec((ROW_TILE, D_MODEL), lambda i: (i, 0)),
                  pl.BlockSpec((D_MODEL, nq), lambda i: (0, 0)),
                  pl.BlockSpec((D_MODEL, nq), lambda i: (0, 0))],
        out_specs=pl.BlockSpec((ROW_TILE, nq), lambda i: (i, 0)),
        out_shape=jax.ShapeDtypeStruct((n, nq), f32),
        compiler_params=_cparams("parallel"),
        name="peer_query",
    )(x, wq_hi, wq_lo)


def _transpose_cast_kernel(x_ref, o_ref):
    o_ref[...] = x_ref[...].T.astype(bf16)


def _transpose_bf16(x):
    e, d = x.shape
    return pl.pallas_call(
        _transpose_cast_kernel,
        grid=(e // PEER_TE,),
        in_specs=[pl.BlockSpec((PEER_TE, d), lambda i: (i, 0))],
        out_specs=pl.BlockSpec((d, PEER_TE), lambda i: (0, i)),
        out_shape=jax.ShapeDtypeStruct((d, e), bf16),
        compiler_params=_cparams("parallel"),
        name="transpose_bf16",
    )(x)


def _top_values(s, n, want_rank=False):
    vals = []
    rank = jnp.full(s.shape, float(n), f32) if want_rank else None
    for j in range(n):
        m = jnp.max(s, axis=0, keepdims=True)
        vals.append(m)
        hit = s == m
        if want_rank:
            rank = jnp.where(hit, float(j), rank)
        s = jnp.where(hit, -jnp.inf, s)
    vals = jnp.concatenate(vals, axis=0)
    return (vals, rank) if want_rank else vals


def _peer_router(s1, s2):
    v1 = _top_values(s1, PEER_TOPK)
    v2, rank2 = _top_values(s2, PEER_TOPK, want_rank=True)
    sub = lax.broadcasted_iota(jnp.int32, (8, 1), 0)
    v2_lo = v2[0:8]
    spans, group = [], []
    for a in range(1, PEER_TOPK):
        nb = PEER_TOPK // (a + 1)
        if sum(n for _, n in group) + nb > 8:
            spans.append(group)
            group = []
        group.append((a, nb))
    spans.append(group)
    cands = [v1[0:1] + v2]
    for group in spans:
        off, packed = 0, None
        for a, nb in group:
            piece = v1[a:a + 1] + (v2_lo if off == 0 else pltpu.roll(v2_lo, off, 0))
            packed = piece if packed is None else jnp.where(sub >= off, piece, packed)
            off += nb
        cands.append(packed if off == 8 else jnp.where(sub >= off, -jnp.inf, packed))
    cand = jnp.concatenate(cands, axis=0)
    seen = jnp.zeros(v1[0:1].shape, f32)
    thr = above = None
    rest = cand
    for _ in range(PEER_TOPK):
        m = jnp.max(rest, axis=0, keepdims=True)
        hit = rest == m
        after = seen + jnp.sum(jnp.where(hit, 1.0, 0.0), axis=0, keepdims=True)
        reach = jnp.logical_and(seen < PEER_TOPK, after >= PEER_TOPK)
        thr = m if thr is None else jnp.where(reach, m, thr)
        above = seen if above is None else jnp.where(reach, seen, above)
        seen = after
        rest = jnp.where(hit, -jnp.inf, rest)
    need = PEER_TOPK - above
    top0 = v1[0:1] + v2[0:1]
    z = (jnp.sum(jnp.where(cand > thr, jnp.exp(cand - top0), 0.0), axis=0, keepdims=True)
         + need * jnp.exp(thr - top0))
    last = PEER_TOPK - 1
    e1 = jnp.where(s1 >= v1[last:last + 1], jnp.exp(s1 - v1[0:1]) / z, 0.0)
    p = jnp.where(s2 >= v2[last:last + 1], jnp.exp(s2 - v2[0:1]), 0.0)
    kappa = jnp.zeros(s1.shape, f32)
    taken = jnp.zeros(seen.shape, f32)
    for a in range(PEER_TOPK):
        sums = v1[a:a + 1] + v2
        greater = jnp.sum(jnp.where(sums > thr, 1.0, 0.0), axis=0, keepdims=True)
        equal = jnp.sum(jnp.where(sums == thr, 1.0, 0.0), axis=0, keepdims=True)
        count = greater + jnp.minimum(equal, jnp.maximum(need - taken, 0.0))
        taken = taken + equal
        kappa = jnp.where(s1 == v1[a:a + 1], count, kappa)
    return e1, kappa, p.astype(bf16), rank2.astype(bf16)


INV_SQRT2 = 1.0 / math.sqrt(2.0)


def _peer_kernel(alpha, n_tiles, n_chunks, x_ref, xprev_ref, q_ref, keys_ref, u_ref, vt_ref, ln_ref, o_ref,
                 xt_scr, e1_scr, kap_scr, p_scr, rank_scr, acc_scr, g_scr, ht_scr, w_scr):
    i = pl.program_id(0)
    j = pl.program_id(1)
    tn = x_ref.shape[0]
    assert n_chunks % 2 == 0 and n_chunks >= 2

    @pl.when(jnp.logical_and(i == 0, j == 0))
    def _():
        acc_scr[...] = jnp.zeros(acc_scr.shape, f32)
        g_scr[...] = jnp.zeros(g_scr.shape, bf16)
        ht_scr[...] = jnp.zeros(ht_scr.shape, f32)
        w_scr[...] = jnp.zeros(w_scr.shape, bf16)

    @pl.when(jnp.logical_and(i < n_tiles, j == 0))
    def _():
        xt_scr[...] = x_ref[...].T.astype(bf16)
        acc_scr[i % 2] = jnp.zeros(acc_scr.shape[1:], f32)

        def route(h, carry):
            col = pl.multiple_of(h * 2 * PEER_HALF, 2 * PEER_HALF)
            q1 = q_ref[:, pl.ds(col, PEER_HALF)]
            q2 = q_ref[:, pl.ds(col + PEER_HALF, PEER_HALF)]
            s1 = _dot3(keys_ref[2 * h], q1, NT)
            s2 = _dot3(keys_ref[2 * h + 1], q2, NT)
            e1, kappa, p, rank2 = _peer_router(s1, s2)
            e1_scr[h] = e1
            kap_scr[h] = kappa
            p_scr[h] = p
            rank_scr[h] = rank2
            return carry

        lax.fori_loop(0, PEER_HEADS, route, 0)

    cur, prev = j % 2, (j + 1) % 2
    acc_slot = jnp.where(j >= 2, i, i + 1) % 2
    hh = ht_scr[prev]
    g_scr[prev] = w_scr[prev] * (0.5 * hh * (1.0 + lax.erf(hh * INV_SQRT2))).astype(bf16)
    ht_scr[cur] = _dot(u_ref[...], xt_scr[...])
    acc_scr[acc_slot] += _dot(vt_ref[...], g_scr[cur])
    n_rows = PEER_TE // PEER_NKEYS
    zero = jnp.zeros((), bf16)
    for rr in range(n_rows):
        r = j * n_rows + rr
        w = jnp.zeros((PEER_NKEYS, tn), bf16)
        for h in range(PEER_HEADS):
            sel = rank_scr[h] < kap_scr[h, pl.ds(r, 1), :].astype(bf16)
            w = w + jnp.where(sel, p_scr[h], zero) * e1_scr[h, pl.ds(r, 1), :].astype(bf16)
        w_scr[cur, rr * PEER_NKEYS:(rr + 1) * PEER_NKEYS, :] = w

    @pl.when(jnp.logical_and(i > 0, j == 1))
    def _():
        y = acc_scr[(i + 1) % 2].T
        o_ref[...] = _layer_norm(alpha * xprev_ref[...] + y, ln_ref[0:1, :], ln_ref[1:2, :])


def _peer_ffn(x, wq_hi, wq_lo, keys, u_bf, vt_bf, ln, alpha):
    n = x.shape[0]
    n_exp = u_bf.shape[0]
    assert n % PEER_TN == 0 and n_exp == PEER_NKEYS * PEER_NKEYS and n_exp % PEER_TE == 0
    n_chunks = n_exp // PEER_TE
    q = _peer_query(x, wq_hi, wq_lo)
    nq = q.shape[1]
    head_scr = pltpu.VMEM((PEER_HEADS, PEER_NKEYS, PEER_TN), f32)
    head_scr16 = pltpu.VMEM((PEER_HEADS, PEER_NKEYS, PEER_TN), bf16)
    n_tiles = n // PEER_TN
    this_tile = lambda i, j: (jnp.minimum(i, n_tiles - 1), 0)
    out_tile = lambda i, j: (jnp.clip(jnp.where(j >= 2, i, i - 1), 0, n_tiles - 1), 0)
    return pl.pallas_call(
        functools.partial(_peer_kernel, alpha, n_tiles, n_chunks),
        grid=(n_tiles + 1, n_chunks),
        in_specs=[
            pl.BlockSpec((PEER_TN, D_MODEL), this_tile),
            pl.BlockSpec((PEER_TN, D_MODEL), lambda i, j: (jnp.maximum(i - 1, 0), 0)),
            pl.BlockSpec((PEER_TN, nq), this_tile),
            pl.BlockSpec((2 * PEER_HEADS, PEER_NKEYS, PEER_HALF), lambda i, j: (0, 0, 0)),
            pl.BlockSpec((PEER_TE, D_MODEL), lambda i, j: (j, 0)),
            pl.BlockSpec((D_MODEL, PEER_TE), lambda i, j: (0, (j + n_chunks - 2) % n_chunks)),
            pl.BlockSpec((2, D_MODEL), lambda i, j: (0, 0)),
        ],
        out_specs=pl.BlockSpec((PEER_TN, D_MODEL), out_tile),
        out_shape=jax.ShapeDtypeStruct((n, D_MODEL), f32),
        scratch_shapes=[pltpu.VMEM((D_MODEL, PEER_TN), bf16), head_scr, head_scr, head_scr16, head_scr16,
                        pltpu.VMEM((2, D_MODEL, PEER_TN), f32), pltpu.VMEM((2, PEER_TE, PEER_TN), bf16),
                        pltpu.VMEM((2, PEER_TE, PEER_TN), f32), pltpu.VMEM((2, PEER_TE, PEER_TN), bf16)],
        compiler_params=_cparams("arbitrary", "arbitrary"),
        name="peer_ffn",
    )(x, x, q, keys, u_bf, vt_bf, ln)


POOL_TM = 256


def kernel(x_prompt, x_sample, cache_k, cache_v, cache_logf, page_table, state_wkv, state_shift, state_pool,
           meta, w_in, b_f, w_o, mu_shift, w0, w2, a0, a2, g2, k_k, k_a, r_k, gn_g, gn_b, w_pool, pool_scale,
           ln_g, ln_b, peer_wq, peer_keys, peer_u, peer_v):
    prm = dict(mu_shift=mu_shift, w0=w0, w2=w2, a0=a0, a2=a2, g2=g2, k_k=k_k, k_a=k_a, r_k=r_k,
               gn_g=gn_g, gn_b=gn_b)
    bp, seq, d = x_prompt.shape
    db, dec_seq, _ = x_sample.shape
    assert bp == 1 and d == D_MODEL
    depth = ln_g.shape[0]
    alpha = (2 * depth) ** 0.25
    n_sample = db * dec_seq
    t_p = seq + N_META
    n_valid = n_sample + t_p
    n_rows = n_sample + -(-t_p // FOX_TQ) * FOX_TQ
    n_rows = -(-n_rows // ROW_TILE) * ROW_TILE
    h = jnp.concatenate([x_sample.reshape(n_sample, d), meta.astype(f32), x_prompt[0],
                         jnp.zeros((n_rows - n_valid, d), f32)], axis=0)

    ks, vs, lfs, wkvs, shifts, pools = ([[], []] for _ in range(6))
    n_qkv = A_IN + 3 * D_B
    for i in range(depth):
        j = i // 2
        ln1 = jnp.stack([ln_g[i, 0], ln_b[i, 0]])
        ln2 = jnp.stack([ln_g[i, 1], ln_b[i, 1]])
        if i % 2 == 0:
            w_cat = jnp.concatenate([w_in[j][:, :n_qkv],
                                     jnp.pad(w_in[j][:, n_qkv:], ((0, 0), (0, LANES - H_B)))], axis=1).astype(bf16)
            b_pad = jnp.pad(b_f[j], (0, LANES - H_B)).reshape(1, LANES)
            pa, kv, qkv16, lf = _in_projection(h, w_cat, b_pad)
            ya, wkv_p, wkv_s = _rwkv_mixer(pa, state_shift[j], state_wkv[j], prm, j, n_sample, n_valid, dec_seq)
            yb = _fox_attention(qkv16, kv, lf, cache_k[j], cache_v[j], cache_logf[j], page_table,
                                n_sample, n_valid, dec_seq)
            h1 = _out_projection(ya, yb, h, w_o[j].astype(bf16), ln1, alpha)
            for grp, lo, hi, lead in ((0, n_sample, n_valid, (bp, t_p)), (1, 0, n_sample, (db, dec_seq))):
                ks[grp].append(kv[lo:hi, :D_B].reshape(lead + (H_B, HEAD_DIM)))
                vs[grp].append(kv[lo:hi, D_B:].reshape(lead + (H_B, HEAD_DIM)))
                lfs[grp].append(lf[lo:hi, :H_B].reshape(lead + (H_B,)))
                shifts[grp].append(pa[lo:hi].reshape(lead + (A_IN,))[:, -1])
            wkvs[0].append(wkv_p)
            wkvs[1].append(wkv_s)
        else:
            ext_s = jnp.concatenate([jnp.zeros((db, POOL_HIST - POOL_BUF, d), f32), state_pool[j],
                                     h[:n_sample].reshape(db, dec_seq, d)], axis=1)
            pools[0].append(h[n_valid - POOL_BUF:n_valid][None])
            pools[1].append(ext_s[:, -POOL_BUF:])
            w_bf = w_pool[j].astype(bf16)
            scale = pool_scale[j].reshape(1, d)
            first = n_sample // POOL_TM
            hp = _pool_mixer(h, w_bf, scale, ln1, alpha, POOL_TM, first, n_rows // POOL_TM - first, True, n_sample)
            ext_len = ext_s.shape[1]
            hs = _pool_mixer(ext_s.reshape(db * ext_len, d), w_bf, scale, ln1, alpha, db * ext_len, 0, 1, False, 0)
            hs = hs.reshape(db, ext_len, d)[:, POOL_HIST:].reshape(n_sample, d)
            h1 = jnp.concatenate([hs, hp[n_sample:]], axis=0)
        wq = peer_wq[i]
        wq_hi = wq.astype(bf16)
        wq_lo = (wq - wq_hi.astype(f32)).astype(bf16)
        h = _peer_ffn(h1, wq_hi, wq_lo, peer_keys[i].reshape(2 * PEER_HEADS, PEER_NKEYS, PEER_HALF),
                      peer_u[i].astype(bf16), _transpose_bf16(peer_v[i]), ln2, alpha)

    y_prompt = h[n_sample + N_META:n_valid][None]
    y_sample = h[:n_sample].reshape(db, dec_seq, d)
    st = lambda xs: jnp.stack(xs)
    return (y_prompt, y_sample,
            st(ks[0]), st(vs[0]), st(lfs[0]), st(wkvs[0]), st(shifts[0]), st(pools[0]),
            st(ks[1]), st(vs[1]), st(lfs[1]), st(wkvs[1]), st(shifts[1]), st(pools[1]))
```

```python
import functools
import math

import jax
import jax.numpy as jnp
from jax import lax
from jax.experimental import pallas as pl
from jax.experimental.pallas import tpu as pltpu

f32 = jnp.float32
bf16 = jnp.bfloat16

D_MODEL = 1024
N_META = 16
HEAD_DIM = 64
D_A = D_MODEL // 2
D_B = D_MODEL - D_A
H_A = D_A // HEAD_DIM
H_B = D_B // HEAD_DIM
LORA_W = 64
LORA_A = 64
LORA_G = 128
A_K = D_A
A_V = 2 * D_A
A_W = 3 * D_A
A_IN = A_W + LORA_W + LORA_A + LORA_G
GN_EPS = HEAD_DIM * 1e-5
POOL_WINDOWS = (2, 4, 8, 16)
POOL_G = D_MODEL // len(POOL_WINDOWS)
POOL_BUF = max(POOL_WINDOWS) - 1
PEER_HEADS = 8
PEER_NKEYS = 128
PEER_TOPK = 16
PEER_HALF = 128
LN_EPS = 1e-5
NEG = -1e30

LANES = 128
PAIR = 2 * HEAD_DIM
N_PAIR = D_A // PAIR
ROW_TILE = 512
SCAN_C = 64
INV_BLOCK = 16
FOX_TQ = 256
FOX_TK = 256
PAGES_PER_STEP = 32
PEER_TN = 256
PEER_TE = 2048
VMEM_LIMIT = 56 * 1024 * 1024


def _cparams(*sem):
    return pltpu.CompilerParams(dimension_semantics=sem, vmem_limit_bytes=VMEM_LIMIT)


def _dot(a, b, dims=(((1,), (0,)), ((), ()))):
    return lax.dot_general(a, b, dims, preferred_element_type=f32)


NT = (((1,), (1,)), ((), ()))
TN = (((0,), (0,)), ((), ()))
NN = (((1,), (0,)), ((), ()))


def _split(x):
    hi = x.astype(bf16)
    lo = (x - hi.astype(f32)).astype(bf16)
    return hi, lo


def _dot3(a, b, dims=NN):
    ah, al = _split(a)
    bh, bl = _split(b)
    return _dot(ah, bh, dims) + (_dot(al, bh, dims) + _dot(ah, bl, dims))


def _dot_exact_rhs(a, b_bf, dims=NN):
    a0 = a.astype(bf16)
    r1 = a - a0.astype(f32)
    a1 = r1.astype(bf16)
    a2 = (r1 - a1.astype(f32)).astype(bf16)
    return _dot(a0, b_bf, dims) + (_dot(a1, b_bf, dims) + _dot(a2, b_bf, dims))


def _dot_exact_lhs(a_bf, b, dims=NN):
    b0 = b.astype(bf16)
    r1 = b - b0.astype(f32)
    b1 = r1.astype(bf16)
    b2 = (r1 - b1.astype(f32)).astype(bf16)
    return _dot(a_bf, b0, dims) + (_dot(a_bf, b1, dims) + _dot(a_bf, b2, dims))


def _layer_norm(x, g, b):
    mu = jnp.mean(x, axis=-1, keepdims=True)
    xc = x - mu
    var = jnp.mean(xc * xc, axis=-1, keepdims=True)
    return xc * lax.rsqrt(var + LN_EPS) * g + b


def _head_ones():
    r = lax.broadcasted_iota(jnp.int32, (D_A, D_A), 0) // HEAD_DIM
    c = lax.broadcasted_iota(jnp.int32, (D_A, D_A), 1) // HEAD_DIM
    return jnp.where(r == c, 1.0, 0.0).astype(bf16)


def _proj_kernel(x_ref, w_ref, bf_ref, pa_ref, kv_ref, qkv16_ref, lf_ref):
    x = x_ref[...].astype(bf16)
    y = _dot(x, w_ref[...])
    pa_ref[...] = y[:, :A_IN]
    q = y[:, A_IN:A_IN + D_B]
    kv = y[:, A_IN + D_B:A_IN + 3 * D_B]
    kv_ref[...] = kv
    qkv16_ref[:, :D_B] = (q * (HEAD_DIM ** -0.5)).astype(bf16)
    qkv16_ref[:, D_B:] = kv.astype(bf16)
    lf_ref[...] = jax.nn.log_sigmoid(y[:, A_IN + 3 * D_B:] + bf_ref[...])


def _in_projection(x, w_bf, b_f_pad):
    n = x.shape[0]
    nw = w_bf.shape[1]
    return pl.pallas_call(
        _proj_kernel,
        grid=(n // ROW_TILE,),
        in_specs=[
            pl.BlockSpec((ROW_TILE, D_MODEL), lambda i: (i, 0)),
            pl.BlockSpec((D_MODEL, nw), lambda i: (0, 0)),
            pl.BlockSpec((1, LANES), lambda i: (0, 0)),
        ],
        out_specs=[
            pl.BlockSpec((ROW_TILE, A_IN), lambda i: (i, 0)),
            pl.BlockSpec((ROW_TILE, 2 * D_B), lambda i: (i, 0)),
            pl.BlockSpec((ROW_TILE, 3 * D_B), lambda i: (i, 0)),
            pl.BlockSpec((ROW_TILE, LANES), lambda i: (i, 0)),
        ],
        out_shape=[
            jax.ShapeDtypeStruct((n, A_IN), f32),
            jax.ShapeDtypeStruct((n, 2 * D_B), f32),
            jax.ShapeDtypeStruct((n, 3 * D_B), bf16),
            jax.ShapeDtypeStruct((n, LANES), f32),
        ],
        compiler_params=_cparams("parallel"),
        name="in_projection",
    )(x, w_bf, b_f_pad)


PREP_TM = 256


def _prep_kernel(n_sample, n_valid, dec_seq,
                 pa_ref, prev_ref, ss_ref, mu_ref, vec_ref, w2_ref, a2_ref, g2_ref,
                 r_o, k_o, v_o, kk_o, b_o, lw_o, g_o, bonus_o):
    i = pl.program_id(0)
    tm = pa_ref.shape[0]
    pa = pa_ref[...]
    local = lax.broadcasted_iota(jnp.int32, (tm, 1), 0)
    row = i * tm + local
    prev = pltpu.roll(pa, 1, 0)
    prev = jnp.where(local == 0, prev_ref[7:8, :], prev)
    prev = jnp.where(row == n_sample, 0.0, prev)
    seq_start = jnp.logical_and(row < n_sample, row % dec_seq == 0)
    prev = jnp.where(seq_start, ss_ref[...], prev)
    xs = pa + (prev - pa) * mu_ref[...]

    w0 = vec_ref[0:1, :]
    a0 = vec_ref[1:2, :]
    k_k = vec_ref[2:3, :]
    k_a = vec_ref[3:4, :]
    r_k = vec_ref[4:5, :]
    r = xs[:, :A_K]
    k = xs[:, A_K:A_V]
    v = xs[:, A_V:A_W]
    wa = xs[:, A_W:A_W + LORA_W + LORA_A]
    xg = xs[:, A_W + LORA_W + LORA_A:]
    w = -jax.nn.softplus(-(w0 + _dot3(jnp.tanh(wa), w2_ref[...]))) - 0.5
    lw = -jnp.exp(w)
    a = jax.nn.sigmoid(a0 + _dot3(wa, a2_ref[...]))
    g = _dot(jax.nn.sigmoid(xg).astype(bf16), g2_ref[...].astype(bf16))
    ones = _head_ones()
    kk = k * k_k
    kk = kk / jnp.maximum(jnp.sqrt(_dot_exact_rhs(kk * kk, ones)), 1e-12)
    k = k * (1.0 + (a - 1.0) * k_a)
    bonus = _dot_exact_rhs(r * k * r_k, ones) * v
    valid = row < n_valid
    r_o[...] = r
    k_o[...] = jnp.where(valid, k, 0.0)
    v_o[...] = jnp.where(valid, v, 0.0)
    kk_o[...] = kk
    b_o[...] = jnp.where(valid, kk * a, 0.0)
    lw_o[...] = jnp.where(valid, lw, 0.0)
    g_o[...] = g
    bonus_o[...] = bonus


def _rwkv_prep(pa, ss_rows, mu, vecs, w2p, a2p, g2, n_sample, n_valid, dec_seq):
    n = pa.shape[0]
    assert n_sample == PREP_TM and n % PREP_TM == 0
    row_spec = pl.BlockSpec((PREP_TM, D_A), lambda i: (i, 0))
    const = lambda shape: pl.BlockSpec(shape, lambda i: (0, 0))
    return pl.pallas_call(
        functools.partial(_prep_kernel, n_sample, n_valid, dec_seq),
        grid=(n // PREP_TM,),
        in_specs=[
            pl.BlockSpec((PREP_TM, A_IN), lambda i: (i, 0)),
            pl.BlockSpec((8, A_IN), lambda i: (jnp.maximum(i * (PREP_TM // 8) - 1, 0), 0)),
            const((PREP_TM, A_IN)),
            const((1, A_IN)),
            const((8, D_A)),
            const((LORA_W + LORA_A, D_A)),
            const((LORA_W + LORA_A, D_A)),
            const((LORA_G, D_A)),
        ],
        out_specs=[row_spec] * 8,
        out_shape=[jax.ShapeDtypeStruct((n, D_A), f32)] * 8,
        compiler_params=_cparams("parallel"),
        name="rwkv_prep",
    )(pa, pa, ss_rows, mu, vecs, w2p, a2p, g2)


def _stack_heads(x, lane_a):
    return jnp.concatenate([jnp.where(lane_a, x, 0.0), jnp.where(lane_a, 0.0, x)], axis=0)


def _unit_lower_inverse(x):
    n = x.shape[0]
    ri = lax.broadcasted_iota(jnp.int32, (n, n), 0)
    ci = lax.broadcasted_iota(jnp.int32, (n, n), 1)
    eye = jnp.where(ri == ci, 1.0, 0.0)
    d = jnp.where(ri // INV_BLOCK == ci // INV_BLOCK, x, 0.0)
    off = x - d
    d2 = _dot3(d, d)
    yield
    d4 = _dot3(d2, d2)
    yield
    d8 = _dot3(d4, d4)
    p = eye - d
    p = p + _dot3(p, d2)
    yield
    p = p + _dot3(p, d4)
    yield
    dinv = p + _dot3(p, d8)
    yield
    m = _dot3(dinv, off)
    yield
    m2 = _dot3(m, m)
    yield
    q = eye - m
    q = q + _dot3(q, m2)
    yield
    return _dot3(q, dinv)


def _scan_pair(r, k, v, kk, b, lw, g, bonus, gn, get_state):
    C = r.shape[0]
    assert C == SCAN_C and SCAN_C // INV_BLOCK == 4 and 2 * C == PAIR
    lane = lax.broadcasted_iota(jnp.int32, (1, PAIR), 1)
    lane_a = lane < HEAD_DIM
    t_idx = lax.broadcasted_iota(jnp.int32, (C, 1), 0)
    tri = jnp.where(lax.broadcasted_iota(jnp.int32, (C, C), 1) <= t_idx, 1.0, 0.0).astype(bf16)
    cs = _dot_exact_lhs(tri, lw)
    total = cs[C - 1:C, :]
    kp = kk * jnp.exp(cs - lw)
    rp = r * jnp.exp(cs)
    g_inv = jnp.exp(-cs)
    g_tail = jnp.exp(total - cs)
    kt, bt = k * g_inv, b * g_inv
    kh, bh = k * g_tail, b * g_tail

    st = lambda x: _stack_heads(x, lane_a)
    sc = _dot3(jnp.concatenate([kp, rp], axis=0),
               jnp.concatenate([st(bt), st(kt)], axis=0), NT)
    yield
    s_idx = lane % C
    strict = s_idx < t_idx
    incl = s_idx <= t_idx
    lb = jnp.where(strict, sc[:C, :PAIR], 0.0)
    lk = jnp.where(strict, sc[:C, PAIR:], 0.0)
    pb = jnp.where(incl, sc[C:, :PAIR], 0.0)
    pk = jnp.where(incl, sc[C:, PAIR:], 0.0)

    lkv = _dot3(lk, st(v))
    yield
    tinv = yield from _unit_lower_inverse(st(lb))
    yield
    tcat = tinv[:C] + tinv[C:]
    w12 = _dot3(tcat, jnp.concatenate([st(lkv), st(kp)], axis=1))
    yield
    w1, w2 = w12[:, :PAIR], w12[:, PAIR:]
    pbw = _dot(pb.astype(bf16), jnp.concatenate([st(w1), st(w2)], axis=1).astype(bf16))
    pkv = _dot(pk.astype(bf16), st(v).astype(bf16))
    yield
    S = get_state()
    q2 = rp - pbw[:, PAIR:]
    y = _dot(q2.astype(bf16), S.astype(bf16), NT) + (pkv - pbw[:, :PAIR])

    ri = lax.broadcasted_iota(jnp.int32, (PAIR, PAIR), 0)
    ci = lax.broadcasted_iota(jnp.int32, (PAIR, PAIR), 1)
    same_head = (ri // HEAD_DIM) == (ci // HEAD_DIM)
    wb = _dot3(jnp.concatenate([w2, w1], axis=1), bh, TN)
    yield
    g2m = jnp.where(ri == ci, jnp.exp(total), 0.0) - jnp.where(same_head, wb[:PAIR], 0.0)
    g1m = jnp.where(same_head, _dot3(v, kh, TN) - wb[PAIR:], 0.0)
    yield
    s_new = _dot3(S, g2m) + g1m
    yield

    ones = jnp.where(same_head, 1.0, 0.0).astype(bf16)
    mean = _dot_exact_rhs(y, ones) * (1.0 / HEAD_DIM)
    yc = y - mean
    var = _dot_exact_rhs(yc * yc, ones) * (1.0 / HEAD_DIM)
    yn = yc * lax.rsqrt(var + GN_EPS) * gn[0:1, :] + gn[1:2, :]
    return (yn + bonus) * g, s_new


SCAN_STEP_CHUNKS = 2
SCAN_LAG = 4


def _scan_kernel(per_chunk_state, *refs):
    if per_chunk_state:
        (r_ref, k_ref, v_ref, kk_ref, b_ref, lw_ref, g_ref, bonus_ref, gn_ref, s0_ref,
         ya_ref, sout_ref, s_scr) = refs
    else:
        (r_ref, k_ref, v_ref, kk_ref, b_ref, lw_ref, g_ref, bonus_ref, gn_ref,
         ya_ref, sout_ref, s_scr) = refs

        @pl.when(pl.program_id(0) == 0)
        def _():
            s_scr[...] = jnp.zeros(s_scr.shape, f32)

    row_refs = (r_ref, k_ref, v_ref, kk_ref, b_ref, lw_ref, g_ref, bonus_ref)
    done = {}

    def incoming(cc, p):
        if per_chunk_state:
            return lambda: s0_ref[cc, p]
        return (lambda: s_scr[p]) if cc == 0 else (lambda: done[cc - 1, p][1])

    chains = {}
    for cc in range(SCAN_STEP_CHUNKS):
        rows = slice(cc * SCAN_C, (cc + 1) * SCAN_C)
        for p in range(N_PAIR):
            sl = slice(p * PAIR, (p + 1) * PAIR)
            chains[cc, p] = _scan_pair(*(x[rows, sl] for x in row_refs), gn_ref[:, sl], incoming(cc, p))
    rnd = 0
    while chains:
        for (cc, p) in sorted(chains):
            if rnd < cc * SCAN_LAG:
                continue
            try:
                next(chains[cc, p])
            except StopIteration as stop:
                ya, s_new = done[cc, p] = stop.value
                ya_ref[cc * SCAN_C:(cc + 1) * SCAN_C, p * PAIR:(p + 1) * PAIR] = ya
                if per_chunk_state:
                    sout_ref[cc, p] = s_new
                elif cc == SCAN_STEP_CHUNKS - 1:
                    s_scr[p] = s_new
                    sout_ref[0, p] = s_new
                del chains[cc, p]
        rnd += 1


def _rwkv_scan(arrs, gn, n_rows_out, row_block_off, n_chunks, s0=None):
    per_chunk = s0 is not None
    assert n_chunks % SCAN_STEP_CHUNKS == 0
    blk = pl.BlockSpec((SCAN_STEP_CHUNKS * SCAN_C, D_A), lambda c: (c + row_block_off, 0))
    in_specs = [blk] * 8 + [pl.BlockSpec((2, D_A), lambda c: (0, 0))]
    args = list(arrs) + [gn]
    if per_chunk:
        n_states, state_block = n_chunks, SCAN_STEP_CHUNKS
        s_map = lambda c: (c, 0, 0, 0)
        in_specs.append(pl.BlockSpec((state_block, N_PAIR, PAIR, PAIR), s_map))
        args.append(s0)
    else:
        n_states, state_block = 1, 1
        s_map = lambda c: (0, 0, 0, 0)
    return pl.pallas_call(
        functools.partial(_scan_kernel, per_chunk),
        grid=(n_chunks // SCAN_STEP_CHUNKS,),
        in_specs=in_specs,
        out_specs=[blk, pl.BlockSpec((state_block, N_PAIR, PAIR, PAIR), s_map)],
        out_shape=[jax.ShapeDtypeStruct((n_rows_out, D_A), f32),
                   jax.ShapeDtypeStruct((n_states, N_PAIR, PAIR, PAIR), f32)],
        scratch_shapes=[pltpu.VMEM((N_PAIR, PAIR, PAIR), f32)],
        compiler_params=_cparams("arbitrary"),
        name="rwkv_scan",
    )(*args)


def _pair_states(state):
    b = state.shape[0]
    s = state.reshape(b, N_PAIR, 2, HEAD_DIM, HEAD_DIM)
    out = jnp.zeros((b, N_PAIR, 2, HEAD_DIM, 2, HEAD_DIM), f32)
    out = out.at[:, :, 0, :, 0, :].set(s[:, :, 0]).at[:, :, 1, :, 1, :].set(s[:, :, 1])
    return out.reshape(b, N_PAIR, PAIR, PAIR)


def _unpair_states(s):
    b = s.shape[0]
    s = s.reshape(b, N_PAIR, 2, HEAD_DIM, 2, HEAD_DIM)
    return jnp.stack([s[:, :, 0, :, 0, :], s[:, :, 1, :, 1, :]], axis=2).reshape(b, H_A, HEAD_DIM, HEAD_DIM)


def _rwkv_mixer(pa, state_shift, state_wkv, prm, j, n_sample, n_valid, dec_seq):
    n = pa.shape[0]
    db = n_sample // dec_seq
    ss_rows = jnp.zeros((db, dec_seq, A_IN), f32).at[:, 0].set(state_shift).reshape(n_sample, A_IN)
    vecs = jnp.zeros((8, D_A), f32)
    for idx, name in enumerate(("w0", "a0", "k_k", "k_a", "r_k")):
        vecs = vecs.at[idx].set(prm[name][j].reshape(D_A))
    zpad = jnp.zeros((LORA_W, D_A), f32)
    w2p = jnp.concatenate([prm["w2"][j], zpad], axis=0)
    a2p = jnp.concatenate([zpad, prm["a2"][j]], axis=0)
    arrs = _rwkv_prep(pa, ss_rows, prm["mu_shift"][j].reshape(1, A_IN), vecs, w2p, a2p, prm["g2"][j],
                      n_sample, n_valid, dec_seq)
    gn = jnp.stack([prm["gn_g"][j], prm["gn_b"][j]])
    step_rows = SCAN_STEP_CHUNKS * SCAN_C
    n_chunks = -(-(n_valid - n_sample) // step_rows) * SCAN_STEP_CHUNKS
    assert n_sample % step_rows == 0 and n_sample + n_chunks * SCAN_C <= n and dec_seq <= SCAN_C
    assert db % SCAN_STEP_CHUNKS == 0
    ya_p, s_p = _rwkv_scan(arrs, gn, n, n_sample // step_rows, n_chunks)

    def pad_seq(x):
        x = x[:n_sample].reshape(db, dec_seq, D_A)
        return jnp.pad(x, ((0, 0), (0, SCAN_C - dec_seq), (0, 0))).reshape(db * SCAN_C, D_A)

    ya_s, s_s = _rwkv_scan([pad_seq(x) for x in arrs], gn, db * SCAN_C, 0, db, s0=_pair_states(state_wkv))
    ya_s = ya_s.reshape(db, SCAN_C, D_A)[:, :dec_seq].reshape(n_sample, D_A)
    ya = jnp.concatenate([ya_s, ya_p[n_sample:]], axis=0)
    return ya, _unpair_states(s_p[0][None]), _unpair_states(s_s)


CUM_BLOCK = 256


def _upper_ones(n):
    r = lax.broadcasted_iota(jnp.int32, (n, n), 0)
    c = lax.broadcasted_iota(jnp.int32, (n, n), 1)
    return jnp.where(r <= c, 1.0, 0.0).astype(bf16)


def _cumsum_kernel(x_ref, d_ref, c_ref, carry):
    @pl.when(pl.program_id(0) == 0)
    def _():
        carry[...] = jnp.zeros_like(carry)

    x = x_ref[...]
    n = x.shape[1]
    d_ref[...] = _dot_exact_rhs(x, _upper_ones(n))
    c_ref[0] = carry[...]
    carry[...] = carry[...] + _dot_exact_rhs(x, jnp.ones((n, LANES), bf16))


def _logf_cumsum(lf_t, first_block):
    h, n = lf_t.shape
    nb = n // CUM_BLOCK
    return pl.pallas_call(
        _cumsum_kernel,
        grid=(nb - first_block,),
        in_specs=[pl.BlockSpec((h, CUM_BLOCK), lambda i: (0, i + first_block))],
        out_specs=[pl.BlockSpec((h, CUM_BLOCK), lambda i: (0, i + first_block)),
                   pl.BlockSpec((1, h, LANES), lambda i: (i + first_block, 0, 0))],
        out_shape=[jax.ShapeDtypeStruct((h, n), f32), jax.ShapeDtypeStruct((nb, h, LANES), f32)],
        scratch_shapes=[pltpu.VMEM((h, LANES), f32)],
        compiler_params=_cparams("arbitrary"),
        name="logf_cumsum",
    )(lf_t)


FOX_KW = 2 * PAIR


def _fox_prompt_kernel(row0, cref_ref, q_ref, k_ref, va_ref, vb_ref, o_ref, s_scr, acc_scr):
    pair = pl.program_id(0)
    i = pl.program_id(1)
    q = q_ref[...]
    lane = lax.broadcasted_iota(jnp.int32, (FOX_TQ, PAIR), 1)
    lane_a = lane < HEAD_DIM
    zero = jnp.zeros_like(q)

    def query_t(qh, col):
        bias = jnp.where(jnp.logical_or(lane == col, lane == col + 1), -1.0, 0.0)
        qa = jnp.concatenate([qh.astype(f32), bias], axis=1)
        return qa.T.astype(bf16)

    q_t = (query_t(jnp.where(lane_a, q, zero), 0), query_t(jnp.where(lane_a, zero, q), 2))
    v_refs = (va_ref, vb_ref)
    key_idx = lax.broadcasted_iota(jnp.int32, (FOX_TK, 1), 0)
    qry_idx = lax.broadcasted_iota(jnp.int32, (1, FOX_TQ), 1)
    blk0 = row0 // FOX_TK

    def block_start(j):
        return pl.multiple_of(row0 + j * FOX_TK, FOX_TK)

    def scores(j):
        kb = k_ref[pl.ds(block_start(j), FOX_TK), :]
        return tuple(_dot(kb, q_t[hh]) for hh in range(2))

    def block_max(j, s_pair):
        return tuple(jnp.max(s_pair[hh], axis=0, keepdims=True) - cref_ref[(blk0 + j) * H_B + 2 * pair + hh]
                     for hh in range(2))

    def update(j, s_pair, mt_pair, ms):
        ps, ms_new, alphas = [], [], []
        for hh in range(2):
            cref = cref_ref[(blk0 + j) * H_B + 2 * pair + hh]
            m_new = jnp.maximum(ms[hh], mt_pair[hh])
            ps.append(jnp.exp((s_pair[hh] - (m_new + cref)).astype(bf16)))
            ms_new.append(m_new)
            alphas.append(jnp.exp(ms[hh] - m_new))
        start = block_start(j)
        for hh in range(2):
            acc_scr[hh] = acc_scr[hh] * alphas[hh] + _dot(v_refs[hh][pl.ds(start, FOX_TK), :], ps[hh], TN)
        return tuple(ms_new)

    def step(j, state):
        ms, mt_cur = state
        s_cur = (s_scr[0], s_scr[1])
        s_next = scores(j + 1)
        ms = update(j, s_cur, mt_cur, ms)
        s_scr[0], s_scr[1] = s_next
        return ms, block_max(j + 1, s_next)

    neg = jnp.full((1, FOX_TQ), NEG, f32)
    acc_scr[...] = jnp.zeros(acc_scr.shape, f32)
    s0 = scores(0)
    s_scr[0], s_scr[1] = s0
    ms, _ = lax.fori_loop(0, i, step, ((neg, neg), block_max(0, s0)))
    s_last = tuple(jnp.where(key_idx <= qry_idx, s_scr[hh], NEG) for hh in range(2))
    update(i, s_last, block_max(i, s_last), ms)
    acc_a, acc_b = acc_scr[0], acc_scr[1]
    out_t = jnp.concatenate([acc_a[:HEAD_DIM] / acc_a[HEAD_DIM:HEAD_DIM + 1],
                             acc_b[HEAD_DIM:] / acc_b[0:1]], axis=0)
    o_ref[...] = out_t.T


def _fox_prompt(q16, k_aug, v_a, v_b, cref, n_sample, n_valid):
    n = q16.shape[0]
    assert FOX_TQ == FOX_TK == CUM_BLOCK and n_sample % FOX_TQ == 0
    nq = -(-(n_valid - n_sample) // FOX_TQ)
    assert n_sample + nq * FOX_TQ <= n
    qb0 = n_sample // FOX_TQ
    nb = D_B // PAIR
    return pl.pallas_call(
        functools.partial(_fox_prompt_kernel, n_sample),
        grid=(nb, nq),
        in_specs=[
            pl.BlockSpec(memory_space=pltpu.SMEM),
            pl.BlockSpec((FOX_TQ, PAIR), lambda p, i: (i + qb0, p)),
            pl.BlockSpec((n, FOX_KW), lambda p, i: (0, p)),
            pl.BlockSpec((n, PAIR), lambda p, i: (0, p)),
            pl.BlockSpec((n, PAIR), lambda p, i: (0, p)),
        ],
        out_specs=pl.BlockSpec((FOX_TQ, PAIR), lambda p, i: (i + qb0, p)),
        out_shape=jax.ShapeDtypeStruct((n, D_B), f32),
        scratch_shapes=[pltpu.VMEM((2, FOX_TK, FOX_TQ), f32), pltpu.VMEM((2, PAIR, FOX_TQ), f32)],
        compiler_params=_cparams("parallel", "arbitrary"),
        name="fox_prompt",
    )(cref, q16, k_aug, v_a, v_b)


def _fox_sample_kernel(n_steps, page_size, pt_ref, q_ref, kn_ref, vn_ref, lfn_ref, *refs):
    np_ = PAGES_PER_STEP
    k_refs, v_refs, lf_refs = refs[:np_], refs[np_:2 * np_], refs[2 * np_:3 * np_]
    o_ref, m_scr, l_scr, acc_scr, carry_scr = refs[3 * np_:]
    j = pl.program_id(1)
    ds = q_ref.shape[1]
    nrow = H_B * ds

    @pl.when(j == 0)
    def _():
        m_scr[...] = jnp.full(m_scr.shape, NEG, f32)
        l_scr[...] = jnp.zeros(l_scr.shape, f32)
        acc_scr[...] = jnp.zeros(acc_scr.shape, f32)
        carry_scr[...] = jnp.zeros(carry_scr.shape, f32)

    row_h = lax.broadcasted_iota(jnp.int32, (nrow, D_B), 0) // ds
    lane_h = lax.broadcasted_iota(jnp.int32, (nrow, D_B), 1) // HEAD_DIM
    head_sel = row_h == lane_h
    q = q_ref[0]
    qx = jnp.where(head_sel, jnp.broadcast_to(q[None], (H_B, ds, D_B)).reshape(nrow, D_B), 0.0).astype(bf16)

    def attend(kts, vts, lfs, mask):
        nk = page_size * len(kts)
        cum = lfs[0] if len(lfs) == 1 else jnp.concatenate(lfs, axis=1)
        key_lane = lax.broadcasted_iota(jnp.int32, (1, nk), 1)
        shift = 1
        while shift < nk:
            cum = cum + jnp.where(key_lane >= shift, pltpu.roll(cum, shift, 1), 0.0)
            shift *= 2
        run = carry_scr[...]
        carry_scr[...] = run + cum[:, nk - 1:nk]
        cum = cum + run
        cat = lambda xs: (xs[0] if len(xs) == 1 else jnp.concatenate(xs, axis=1)).astype(bf16)
        s = _dot(qx, cat(kts))
        s = s - jnp.broadcast_to(cum[:, None, :], (H_B, ds, nk)).reshape(nrow, nk)
        if mask is not None:
            s = jnp.where(mask, s, NEG)
        m = m_scr[...]
        m_new = jnp.maximum(m, jnp.max(s, axis=1, keepdims=True))
        alpha = jnp.exp(m - m_new)
        p = jnp.exp(s - m_new)
        l_scr[...] = l_scr[...] * alpha + jnp.sum(p, axis=1, keepdims=True)
        acc_scr[...] = acc_scr[...] * alpha + _dot(p.astype(bf16), cat(vts), NT)
        m_scr[...] = m_new

    attend([r[0] for r in k_refs], [r[0] for r in v_refs], [r[0] for r in lf_refs], None)

    @pl.when(j == n_steps - 1)
    def _():
        t_q = lax.broadcasted_iota(jnp.int32, (nrow, page_size), 0) % ds
        t_k = lax.broadcasted_iota(jnp.int32, (nrow, page_size), 1)
        attend([kn_ref[0]], [vn_ref[0]], [lfn_ref[0]], t_k <= t_q)
        o = jnp.where(head_sel, acc_scr[...] / l_scr[...], 0.0)
        out = o[0:ds]
        for h in range(1, H_B):
            out = out + o[h * ds:(h + 1) * ds]
        o_ref[...] = out


def _fox_sample(q_s, kn_t, vn_t, lfn_t, cache_kt, cache_vt, cache_lft, page_table, dec_seq):
    db = q_s.shape[0]
    n_pages = page_table.shape[1]
    page_size = cache_kt.shape[2]
    assert n_pages % PAGES_PER_STEP == 0 and page_size == LANES
    n_steps = n_pages // PAGES_PER_STEP

    def page_map(pp):
        return lambda b, j, pt: (pt[b, j * PAGES_PER_STEP + pp], 0, 0)

    batch3 = lambda shape: pl.BlockSpec((1,) + shape, lambda b, j, pt: (b, 0, 0))
    in_specs = [batch3((dec_seq, D_B)), batch3((D_B, page_size)), batch3((D_B, page_size)),
                batch3((H_B, page_size))]
    in_specs += [pl.BlockSpec((1, D_B, page_size), page_map(pp)) for pp in range(PAGES_PER_STEP)]
    in_specs += [pl.BlockSpec((1, D_B, page_size), page_map(pp)) for pp in range(PAGES_PER_STEP)]
    in_specs += [pl.BlockSpec((1, H_B, page_size), page_map(pp)) for pp in range(PAGES_PER_STEP)]
    nrow = H_B * dec_seq
    grid_spec = pltpu.PrefetchScalarGridSpec(
        num_scalar_prefetch=1,
        grid=(db, n_steps),
        in_specs=in_specs,
        out_specs=pl.BlockSpec((dec_seq, D_B), lambda b, j, pt: (b, 0)),
        scratch_shapes=[pltpu.VMEM((nrow, 1), f32), pltpu.VMEM((nrow, 1), f32),
                        pltpu.VMEM((nrow, D_B), f32), pltpu.VMEM((H_B, 1), f32)],
    )
    return pl.pallas_call(
        functools.partial(_fox_sample_kernel, n_steps, page_size),
        grid_spec=grid_spec,
        out_shape=jax.ShapeDtypeStruct((db * dec_seq, D_B), f32),
        compiler_params=_cparams("parallel", "arbitrary"),
        name="fox_sample",
    )(page_table, q_s, kn_t, vn_t, lfn_t, *([cache_kt] * PAGES_PER_STEP), *([cache_vt] * PAGES_PER_STEP),
      *([cache_lft] * PAGES_PER_STEP))


def _fox_attention(qkv16, kv, lf, cache_k, cache_v, cache_lf, page_table, n_sample, n_valid, dec_seq):
    n = qkv16.shape[0]
    db = n_sample // dec_seq
    lf_t = lf[:, :H_B].T
    delta_t, c_blk = _logf_cumsum(lf_t, n_sample // CUM_BLOCK)
    delta = delta_t.T
    d_hi = delta.astype(bf16)
    d_lo = (delta - d_hi.astype(f32)).astype(bf16)
    k16, v16 = qkv16[:, D_B:2 * D_B], qkv16[:, 2 * D_B:]
    zpad = jnp.zeros((n, FOX_KW - PAIR - 4), bf16)
    parts = []
    for p in range(D_B // PAIR):
        parts += [k16[:, p * PAIR:(p + 1) * PAIR], d_hi[:, 2 * p:2 * p + 1], d_lo[:, 2 * p:2 * p + 1],
                  d_hi[:, 2 * p + 1:2 * p + 2], d_lo[:, 2 * p + 1:2 * p + 2], zpad]
    k_aug = jnp.concatenate(parts, axis=1)
    lane_a = (jnp.arange(D_B) % PAIR) < HEAD_DIM
    one = jnp.ones((), bf16)
    yb_p = _fox_prompt(qkv16, k_aug, jnp.where(lane_a, v16, one), jnp.where(lane_a, one, v16),
                       c_blk[:, :, 0].reshape(-1), n_sample, n_valid)
    n_pool, page_size = cache_k.shape[0], cache_k.shape[1]
    q_s = qkv16[:n_sample, :D_B].astype(f32).reshape(db, dec_seq, D_B)
    pad_keys = lambda x: jnp.pad(jnp.swapaxes(x, 1, 2), ((0, 0), (0, 0), (0, page_size - dec_seq)))
    kn_t = pad_keys(kv[:n_sample, :D_B].reshape(db, dec_seq, D_B))
    vn_t = pad_keys(kv[:n_sample, D_B:].reshape(db, dec_seq, D_B))
    lfn_t = pad_keys(lf[:n_sample, :H_B].reshape(db, dec_seq, H_B))
    to_t = lambda c: jnp.transpose(c, (0, 2, 3, 1)).reshape(n_pool, D_B, page_size)
    yb_s = _fox_sample(q_s, kn_t, vn_t, lfn_t, to_t(cache_k), to_t(cache_v), jnp.swapaxes(cache_lf, 1, 2),
                       page_table, dec_seq)
    return jnp.concatenate([yb_s, yb_p[n_sample:]], axis=0)


def _outproj_kernel(alpha, ya_ref, yb_ref, h_ref, w_ref, ln_ref, o_ref):
    y = _dot(ya_ref[...].astype(bf16), w_ref[:D_A, :]) + _dot(yb_ref[...].astype(bf16), w_ref[D_A:, :])
    o_ref[...] = _layer_norm(alpha * h_ref[...] + y, ln_ref[0:1, :], ln_ref[1:2, :])


def _out_projection(ya, yb, h, w_bf, ln, alpha):
    n = h.shape[0]
    half = pl.BlockSpec((ROW_TILE, D_A), lambda i: (i, 0))
    full = pl.BlockSpec((ROW_TILE, D_MODEL), lambda i: (i, 0))
    return pl.pallas_call(
        functools.partial(_outproj_kernel, alpha),
        grid=(n // ROW_TILE,),
        in_specs=[half, half, full,
                  pl.BlockSpec((D_MODEL, D_MODEL), lambda i: (0, 0)),
                  pl.BlockSpec((2, D_MODEL), lambda i: (0, 0))],
        out_specs=full,
        out_shape=jax.ShapeDtypeStruct((n, D_MODEL), f32),
        compiler_params=_cparams("parallel"),
        name="out_projection",
    )(ya, yb, h, w_bf, ln)


POOL_HIST = POOL_BUF + 1


def _pool_kernel(prompt_mode, zero_hist_tile, row0, alpha, x_ref, hist_ref, w_ref, sc_ref, ln_ref, o_ref):
    i = pl.program_id(0) + zero_hist_tile
    tm = x_ref.shape[0]
    x = x_ref[...]
    hist = jnp.where(i == zero_hist_tile, 0.0, hist_ref[...])
    ext = jnp.concatenate([hist, x], axis=0)
    t = i * tm + lax.broadcasted_iota(jnp.int32, (tm, 1), 0) - row0
    ys = []
    for gi, win in enumerate(POOL_WINDOWS):
        s = ext[:, gi * POOL_G:(gi + 1) * POOL_G]
        span = 1
        while span < win:
            s = s + pltpu.roll(s, span, 0)
            span *= 2
        cnt = jnp.minimum(win, t + 1).astype(f32) if prompt_mode else float(win)
        diff = s[POOL_HIST:] / cnt - x[:, gi * POOL_G:(gi + 1) * POOL_G]
        ys.append(_dot(diff.astype(bf16), w_ref[gi]))
    y = jnp.concatenate(ys, axis=1) * sc_ref[...]
    o_ref[...] = _layer_norm(alpha * x + y, ln_ref[0:1, :], ln_ref[1:2, :])


def _pool_mixer(x, w_bf, scale, ln, alpha, tm, first_tile, n_tiles, prompt_mode, row0):
    n = x.shape[0]
    assert tm % POOL_HIST == 0
    hb = tm // POOL_HIST
    return pl.pallas_call(
        functools.partial(_pool_kernel, prompt_mode, first_tile, row0, alpha),
        grid=(n_tiles,),
        in_specs=[
            pl.BlockSpec((tm, D_MODEL), lambda i: (i + first_tile, 0)),
            pl.BlockSpec((POOL_HIST, D_MODEL), lambda i: (jnp.maximum((i + first_tile) * hb - 1, 0), 0)),
            pl.BlockSpec((len(POOL_WINDOWS), POOL_G, POOL_G), lambda i: (0, 0, 0)),
            pl.BlockSpec((1, D_MODEL), lambda i: (0, 0)),
            pl.BlockSpec((2, D_MODEL), lambda i: (0, 0)),
        ],
        out_specs=pl.BlockSpec((tm, D_MODEL), lambda i: (i + first_tile, 0)),
        out_shape=jax.ShapeDtypeStruct((n, D_MODEL), f32),
        compiler_params=_cparams("parallel"),
        name="pool_mixer",
    )(x, x, w_bf, scale, ln)


def _peer_query_kernel(x_ref, wh_ref, wl_ref, o_ref):
    xh, xl = _split(x_ref[...])
    o_ref[...] = _dot(xh, wh_ref[...]) + (_dot(xl, wh_ref[...]) + _dot(xh, wl_ref[...]))


def _peer_query(x, wq_hi, wq_lo):
    n = x.shape[0]
    nq = wq_hi.shape[1]
    return pl.pallas_call(
        _peer_query_kernel,
        grid=(n // ROW_TILE,),
        in_specs=[pl.BlockSpec((ROW_TILE, D_MODEL), lambda i: (i, 0)),
                  pl.BlockSpec((D_MODEL, nq), lambda i: (0, 0)),
                  pl.BlockSpec((D_MODEL, nq), lambda i: (0, 0))],
        out_specs=pl.BlockSpec((ROW_TILE, nq), lambda i: (i, 0)),
        out_shape=jax.ShapeDtypeStruct((n, nq), f32),
        compiler_params=_cparams("parallel"),
        name="peer_query",
    )(x, wq_hi, wq_lo)


def _transpose_cast_kernel(x_ref, o_ref):
    o_ref[...] = x_ref[...].T.astype(bf16)


def _transpose_bf16(x):
    e, d = x.shape
    return pl.pallas_call(
        _transpose_cast_kernel,
        grid=(e // PEER_TE,),
        in_specs=[pl.BlockSpec((PEER_TE, d), lambda i: (i, 0))],
        out_specs=pl.BlockSpec((d, PEER_TE), lambda i: (0, i)),
        out_shape=jax.ShapeDtypeStruct((d, e), bf16),
        compiler_params=_cparams("parallel"),
        name="transpose_bf16",
    )(x)


def _top_values(s, n, want_rank=False):
    vals = []
    rank = jnp.full(s.shape, float(n), f32) if want_rank else None
    for j in range(n):
        m = jnp.max(s, axis=0, keepdims=True)
        vals.append(m)
        hit = s == m
        if want_rank:
            rank = jnp.where(hit, float(j), rank)
        s = jnp.where(hit, -jnp.inf, s)
    vals = jnp.concatenate(vals, axis=0)
    return (vals, rank) if want_rank else vals


def _peer_router(s1, s2):
    v1 = _top_values(s1, PEER_TOPK)
    v2, rank2 = _top_values(s2, PEER_TOPK, want_rank=True)
    sub = lax.broadcasted_iota(jnp.int32, (8, 1), 0)
    v2_lo = v2[0:8]
    spans, group = [], []
    for a in range(1, PEER_TOPK):
        nb = PEER_TOPK // (a + 1)
        if sum(n for _, n in group) + nb > 8:
            spans.append(group)
            group = []
        group.append((a, nb))
    spans.append(group)
    cands = [v1[0:1] + v2]
    for group in spans:
        off, packed = 0, None
        for a, nb in group:
            piece = v1[a:a + 1] + (v2_lo if off == 0 else pltpu.roll(v2_lo, off, 0))
            packed = piece if packed is None else jnp.where(sub >= off, piece, packed)
            off += nb
        cands.append(packed if off == 8 else jnp.where(sub >= off, -jnp.inf, packed))
    cand = jnp.concatenate(cands, axis=0)
    seen = jnp.zeros(v1[0:1].shape, f32)
    thr = above = None
    rest = cand
    for _ in range(PEER_TOPK):
        m = jnp.max(rest, axis=0, keepdims=True)
        hit = rest == m
        after = seen + jnp.sum(jnp.where(hit, 1.0, 0.0), axis=0, keepdims=True)
        reach = jnp.logical_and(seen < PEER_TOPK, after >= PEER_TOPK)
        thr = m if thr is None else jnp.where(reach, m, thr)
        above = seen if above is None else jnp.where(reach, seen, above)
        seen = after
        rest = jnp.where(hit, -jnp.inf, rest)
    need = PEER_TOPK - above
    top0 = v1[0:1] + v2[0:1]
    z = (jnp.sum(jnp.where(cand > thr, jnp.exp(cand - top0), 0.0), axis=0, keepdims=True)
         + need * jnp.exp(thr - top0))
    last = PEER_TOPK - 1
    e1 = jnp.where(s1 >= v1[last:last + 1], jnp.exp(s1 - v1[0:1]) / z, 0.0)
    p = jnp.where(s2 >= v2[last:last + 1], jnp.exp(s2 - v2[0:1]), 0.0)
    kappa = jnp.zeros(s1.shape, f32)
    taken = jnp.zeros(seen.shape, f32)
    for a in range(PEER_TOPK):
        sums = v1[a:a + 1] + v2
        greater = jnp.sum(jnp.where(sums > thr, 1.0, 0.0), axis=0, keepdims=True)
        equal = jnp.sum(jnp.where(sums == thr, 1.0, 0.0), axis=0, keepdims=True)
        count = greater + jnp.minimum(equal, jnp.maximum(need - taken, 0.0))
        taken = taken + equal
        kappa = jnp.where(s1 == v1[a:a + 1], count, kappa)
    return e1, kappa, p.astype(bf16), rank2.astype(bf16)


INV_SQRT2 = 1.0 / math.sqrt(2.0)


def _peer_kernel(alpha, n_tiles, n_chunks, x_ref, xprev_ref, q_ref, keys_ref, u_ref, vt_ref, ln_ref, o_ref,
                 xt_scr, e1_scr, kap_scr, p_scr, rank_scr, acc_scr, g_scr, ht_scr, w_scr):
    i = pl.program_id(0)
    j = pl.program_id(1)
    tn = x_ref.shape[0]
    assert n_chunks % 2 == 0 and n_chunks >= 2

    @pl.when(jnp.logical_and(i == 0, j == 0))
    def _():
        acc_scr[...] = jnp.zeros(acc_scr.shape, f32)
        g_scr[...] = jnp.zeros(g_scr.shape, bf16)
        ht_scr[...] = jnp.zeros(ht_scr.shape, f32)
        w_scr[...] = jnp.zeros(w_scr.shape, bf16)

    @pl.when(jnp.logical_and(i < n_tiles, j == 0))
    def _():
        xt_scr[...] = x_ref[...].T.astype(bf16)
        acc_scr[i % 2] = jnp.zeros(acc_scr.shape[1:], f32)

        def route(h, carry):
            col = pl.multiple_of(h * 2 * PEER_HALF, 2 * PEER_HALF)
            q1 = q_ref[:, pl.ds(col, PEER_HALF)]
            q2 = q_ref[:, pl.ds(col + PEER_HALF, PEER_HALF)]
            s1 = _dot3(keys_ref[2 * h], q1, NT)
            s2 = _dot3(keys_ref[2 * h + 1], q2, NT)
            e1, kappa, p, rank2 = _peer_router(s1, s2)
            e1_scr[h] = e1
            kap_scr[h] = kappa
            p_scr[h] = p
            rank_scr[h] = rank2
            return carry

        lax.fori_loop(0, PEER_HEADS, route, 0)

    cur, prev = j % 2, (j + 1) % 2
    acc_slot = jnp.where(j >= 2, i, i + 1) % 2
    hh = ht_scr[prev]
    g_scr[prev] = w_scr[prev] * (0.5 * hh * (1.0 + lax.erf(hh * INV_SQRT2))).astype(bf16)
    ht_scr[cur] = _dot(u_ref[...], xt_scr[...])
    acc_scr[acc_slot] += _dot(vt_ref[...], g_scr[cur])
    n_rows = PEER_TE // PEER_NKEYS
    zero = jnp.zeros((), bf16)
    for rr in range(n_rows):
        r = j * n_rows + rr
        w = jnp.zeros((PEER_NKEYS, tn), bf16)
        for h in range(PEER_HEADS):
            sel = rank_scr[h] < kap_scr[h, pl.ds(r, 1), :].astype(bf16)
            w = w + jnp.where(sel, p_scr[h], zero) * e1_scr[h, pl.ds(r, 1), :].astype(bf16)
        w_scr[cur, rr * PEER_NKEYS:(rr + 1) * PEER_NKEYS, :] = w

    @pl.when(jnp.logical_and(i > 0, j == 1))
    def _():
        y = acc_scr[(i + 1) % 2].T
        o_ref[...] = _layer_norm(alpha * xprev_ref[...] + y, ln_ref[0:1, :], ln_ref[1:2, :])


def _peer_ffn(x, wq_hi, wq_lo, keys, u_bf, vt_bf, ln, alpha):
    n = x.shape[0]
    n_exp = u_bf.shape[0]
    assert n % PEER_TN == 0 and n_exp == PEER_NKEYS * PEER_NKEYS and n_exp % PEER_TE == 0
    n_chunks = n_exp // PEER_TE
    q = _peer_query(x, wq_hi, wq_lo)
    nq = q.shape[1]
    head_scr = pltpu.VMEM((PEER_HEADS, PEER_NKEYS, PEER_TN), f32)
    head_scr16 = pltpu.VMEM((PEER_HEADS, PEER_NKEYS, PEER_TN), bf16)
    n_tiles = n // PEER_TN
    this_tile = lambda i, j: (jnp.minimum(i, n_tiles - 1), 0)
    out_tile = lambda i, j: (jnp.clip(jnp.where(j >= 2, i, i - 1), 0, n_tiles - 1), 0)
    return pl.pallas_call(
        functools.partial(_peer_kernel, alpha, n_tiles, n_chunks),
        grid=(n_tiles + 1, n_chunks),
        in_specs=[
            pl.BlockSpec((PEER_TN, D_MODEL), this_tile),
            pl.BlockSpec((PEER_TN, D_MODEL), lambda i, j: (jnp.maximum(i - 1, 0), 0)),
            pl.BlockSpec((PEER_TN, nq), this_tile),
            pl.BlockSpec((2 * PEER_HEADS, PEER_NKEYS, PEER_HALF), lambda i, j: (0, 0, 0)),
            pl.BlockSpec((PEER_TE, D_MODEL), lambda i, j: (j, 0)),
            pl.BlockSpec((D_MODEL, PEER_TE), lambda i, j: (0, (j + n_chunks - 2) % n_chunks)),
            pl.BlockSpec((2, D_MODEL), lambda i, j: (0, 0)),
        ],
        out_specs=pl.BlockSpec((PEER_TN, D_MODEL), out_tile),
        out_shape=jax.ShapeDtypeStruct((n, D_MODEL), f32),
        scratch_shapes=[pltpu.VMEM((D_MODEL, PEER_TN), bf16), head_scr, head_scr, head_scr16, head_scr16,
                        pltpu.VMEM((2, D_MODEL, PEER_TN), f32), pltpu.VMEM((2, PEER_TE, PEER_TN), bf16),
                        pltpu.VMEM((2, PEER_TE, PEER_TN), f32), pltpu.VMEM((2, PEER_TE, PEER_TN), bf16)],
        compiler_params=_cparams("arbitrary", "arbitrary"),
        name="peer_ffn",
    )(x, x, q, keys, u_bf, vt_bf, ln)


POOL_TM = 256


def kernel(x_prompt, x_sample, cache_k, cache_v, cache_logf, page_table, state_wkv, state_shift, state_pool,
           meta, w_in, b_f, w_o, mu_shift, w0, w2, a0, a2, g2, k_k, k_a, r_k, gn_g, gn_b, w_pool, pool_scale,
           ln_g, ln_b, peer_wq, peer_keys, peer_u, peer_v):
    prm = dict(mu_shift=mu_shift, w0=w0, w2=w2, a0=a0, a2=a2, g2=g2, k_k=k_k, k_a=k_a, r_k=r_k,
               gn_g=gn_g, gn_b=gn_b)
    bp, seq, d = x_prompt.shape
    db, dec_seq, _ = x_sample.shape
    assert bp == 1 and d == D_MODEL
    depth = ln_g.shape[0]
    alpha = (2 * depth) ** 0.25
    n_sample = db * dec_seq
    t_p = seq + N_META
    n_valid = n_sample + t_p
    n_rows = n_sample + -(-t_p // FOX_TQ) * FOX_TQ
    n_rows = -(-n_rows // ROW_TILE) * ROW_TILE
    h = jnp.concatenate([x_sample.reshape(n_sample, d), meta.astype(f32), x_prompt[0],
                         jnp.zeros((n_rows - n_valid, d), f32)], axis=0)

    ks, vs, lfs, wkvs, shifts, pools = ([[], []] for _ in range(6))
    n_qkv = A_IN + 3 * D_B
    for i in range(depth):
        j = i // 2
        ln1 = jnp.stack([ln_g[i, 0], ln_b[i, 0]])
        ln2 = jnp.stack([ln_g[i, 1], ln_b[i, 1]])
        if i % 2 == 0:
            w_cat = jnp.concatenate([w_in[j][:, :n_qkv],
                                     jnp.pad(w_in[j][:, n_qkv:], ((0, 0), (0, LANES - H_B)))], axis=1).astype(bf16)
            b_pad = jnp.pad(b_f[j], (0, LANES - H_B)).reshape(1, LANES)
            pa, kv, qkv16, lf = _in_projection(h, w_cat, b_pad)
            ya, wkv_p, wkv_s = _rwkv_mixer(pa, state_shift[j], state_wkv[j], prm, j, n_sample, n_valid, dec_seq)
            yb = _fox_attention(qkv16, kv, lf, cache_k[j], cache_v[j], cache_logf[j], page_table,
                                n_sample, n_valid, dec_seq)
            h1 = _out_projection(ya, yb, h, w_o[j].astype(bf16), ln1, alpha)
            for grp, lo, hi, lead in ((0, n_sample, n_valid, (bp, t_p)), (1, 0, n_sample, (db, dec_seq))):
                ks[grp].append(kv[lo:hi, :D_B].reshape(lead + (H_B, HEAD_DIM)))
                vs[grp].append(kv[lo:hi, D_B:].reshape(lead + (H_B, HEAD_DIM)))
                lfs[grp].append(lf[lo:hi, :H_B].reshape(lead + (H_B,)))
                shifts[grp].append(pa[lo:hi].reshape(lead + (A_IN,))[:, -1])
            wkvs[0].append(wkv_p)
            wkvs[1].append(wkv_s)
        else:
            ext_s = jnp.concatenate([jnp.zeros((db, POOL_HIST - POOL_BUF, d), f32), state_pool[j],
                                     h[:n_sample].reshape(db, dec_seq, d)], axis=1)
            pools[0].append(h[n_valid - POOL_BUF:n_valid][None])
            pools[1].append(ext_s[:, -POOL_BUF:])
            w_bf = w_pool[j].astype(bf16)
            scale = pool_scale[j].reshape(1, d)
            first = n_sample // POOL_TM
            hp = _pool_mixer(h, w_bf, scale, ln1, alpha, POOL_TM, first, n_rows // POOL_TM - first, True, n_sample)
            ext_len = ext_s.shape[1]
            hs = _pool_mixer(ext_s.reshape(db * ext_len, d), w_bf, scale, ln1, alpha, db * ext_len, 0, 1, False, 0)
            hs = hs.reshape(db, ext_len, d)[:, POOL_HIST:].reshape(n_sample, d)
            h1 = jnp.concatenate([hs, hp[n_sample:]], axis=0)
        wq = peer_wq[i]
        wq_hi = wq.astype(bf16)
        wq_lo = (wq - wq_hi.astype(f32)).astype(bf16)
        h = _peer_ffn(h1, wq_hi, wq_lo, peer_keys[i].reshape(2 * PEER_HEADS, PEER_NKEYS, PEER_HALF),
                      peer_u[i].astype(bf16), _transpose_bf16(peer_v[i]), ln2, alpha)

    y_prompt = h[n_sample + N_META:n_valid][None]
    y_sample = h[:n_sample].reshape(db, dec_seq, d)
    st = lambda xs: jnp.stack(xs)
    return (y_prompt, y_sample,
            st(ks[0]), st(vs[0]), st(lfs[0]), st(wkvs[0]), st(shifts[0]), st(pools[0]),
            st(ks[1]), st(vs[1]), st(lfs[1]), st(wkvs[1]), st(shifts[1]), st(pools[1]))
```
